```python
import jax
import jax.numpy as jnp
from jax import lax
import numpy as np

D_MODEL = 1024
BATCH = 16
SEQ = 2048
DEPTH = 4

CHUNK = 64
D_MIX = D_MODEL
W_GROUP = D_MIX // 4
A_HEADS = 4
A_HD = W_GROUP // A_HEADS
CONV_W = 4
B_HEADS = 4
B_HD = W_GROUP // B_HEADS
SG_BLOCK = 128
C_GROUPS = 4
C_GD = W_GROUP // C_GROUPS
POOL_WINDOWS = (2, 4, 8, 16)
D_HEADS = 4
D_HD = W_GROUP // D_HEADS
SB_QBLOCK = 128
D_FF = 4 * D_MODEL
N_MOD = 6
EPS = 1e-6

COL_SIZES = (3 * W_GROUP, A_HEADS, A_HEADS, W_GROUP,
             W_GROUP, W_GROUP,
             W_GROUP,
             3 * W_GROUP)
P_IN = sum(COL_SIZES)

kernel_name = "hybrid_parallel_mixer_trunk"


def rms_norm(x, g):
    xf = x.astype(jnp.float32)
    y = xf * lax.rsqrt(jnp.mean(xf * xf, axis=-1, keepdims=True) + EPS)
    return (y * g.astype(jnp.float32)).astype(x.dtype)


def layer_norm(x, g):
    xf = x.astype(jnp.float32)
    mu = jnp.mean(xf, axis=-1, keepdims=True)
    xc = xf - mu
    y = xc * lax.rsqrt(jnp.mean(xc * xc, axis=-1, keepdims=True) + EPS)
    return (y * g.astype(jnp.float32)).astype(x.dtype)


def l2norm(x):
    return x * lax.rsqrt(jnp.sum(x * x, axis=-1, keepdims=True) + EPS)


def split_cols(p, sizes):
    outs, o = [], 0
    for s in sizes:
        outs.append(p[..., o:o + s])
        o += s
    return outs


def causal_depthwise_conv(x, w):
    return lax.conv_general_dilated(x, w[:, None, :].astype(x.dtype), window_strides=(1,),
                                    padding=((CONV_W - 1, 0),),
                                    dimension_numbers=("NWC", "WIO", "NWC"),
                                    feature_group_count=x.shape[-1])


def chunk_gated_delta_rule(q, k, v, g, beta):
    Bn, S, H, Dk = q.shape
    n = S // CHUNK

    def blocks(t):
        t = t.reshape((Bn, n, CHUNK, H) + t.shape[3:])
        return jnp.moveaxis(t, 3, 1)

    q = blocks(q) * (Dk ** -0.5)
    k = blocks(k)
    v = blocks(v)
    beta = blocks(beta)
    g = jnp.cumsum(blocks(g), axis=-1)
    pos = jnp.arange(CHUNK)
    incl = pos[:, None] >= pos[None, :]
    strict = pos[:, None] > pos[None, :]
    decay = jnp.exp(jnp.where(incl, g[..., :, None] - g[..., None, :], -jnp.inf))
    k_beta = k * beta[..., None]
    nmat = -jnp.where(strict, jnp.einsum('bhnid,bhnjd->bhnij', k_beta, k) * decay, 0.0)
    tmat = jnp.eye(CHUNK, dtype=q.dtype) + nmat
    pw = nmat
    for _ in range(CHUNK.bit_length() - 2):
        pw = pw @ pw
        tmat = tmat + tmat @ pw
    u = tmat @ (v * beta[..., None])
    w = tmat @ (k_beta * jnp.exp(g)[..., None])
    qk = jnp.einsum('bhnid,bhnjd->bhnij', q, k) * decay
    q_dec = q * jnp.exp(g)[..., None]
    g_last = g[..., -1]
    k_dec = k * jnp.exp(g_last[..., None] - g)[..., None]

    def step(state, inp):
        qd, kd, uc, wc, qkc, gl = inp
        v_new = uc - wc @ state
        o = qd @ state + qkc @ v_new
        state = state * jnp.exp(gl)[..., None, None] + jnp.swapaxes(kd, -1, -2) @ v_new
        return state, o

    xs = tuple(jnp.moveaxis(t, 2, 0) for t in (q_dec, k_dec, u, w, qk, g_last))
    state0 = jnp.zeros((Bn, H, Dk, v.shape[-1]), q.dtype)
    _, o = lax.scan(step, state0, xs)
    o = jnp.moveaxis(o, 0, 2)
    return jnp.moveaxis(o, 1, 3).reshape(Bn, S, H, -1)


def gated_deltanet(qkv, a, b, gate, conv_w, a_log, dt_bias, g_out):
    Bn, S, _ = qkv.shape
    y = jax.nn.silu(causal_depthwise_conv(qkv, conv_w)).astype(jnp.float32)
    q, k, v = [t.reshape(Bn, S, A_HEADS, A_HD) for t in jnp.split(y, 3, axis=-1)]
    q, k = l2norm(q), l2norm(k)
    beta = jax.nn.sigmoid(b.astype(jnp.float32))
    g = -jnp.exp(a_log.astype(jnp.float32)) * jax.nn.softplus(a.astype(jnp.float32) + dt_bias.astype(jnp.float32))
    o = chunk_gated_delta_rule(q, k, v, g, beta)
    o = rms_norm(o, g_out.reshape(A_HEADS, A_HD))
    o = o * jax.nn.silu(gate.astype(jnp.float32)).reshape(Bn, S, A_HEADS, A_HD)
    return o.reshape(Bn, S, W_GROUP).astype(qkv.dtype)


def spatial_gating(u, v, g_ln, w_s, b_s):
    Bn, S, _ = u.shape
    v = layer_norm(v.reshape(Bn, S, B_HEADS, B_HD), g_ln.reshape(B_HEADS, B_HD))
    v = v.reshape(Bn, S // SG_BLOCK, SG_BLOCK, B_HEADS, B_HD)
    cpos = jnp.arange(SG_BLOCK) // CHUNK
    mask = cpos[:, None] >= cpos[None, :]
    w_m = jnp.where(mask, w_s, jnp.zeros_like(w_s))
    s = jnp.einsum('hts,bnshd->bnthd', w_m, v) + jnp.transpose(b_s)[None, None, :, :, None]
    return u * s.reshape(Bn, S, W_GROUP)


def multiscale_pool(xc, w_p, s_p):
    Bn, S, _ = xc.shape
    xf = xc.astype(jnp.float32).reshape(Bn, S, C_GROUPS, C_GD)
    cs = jnp.pad(jnp.cumsum(xf, axis=1), ((0, 0), (1, 0), (0, 0), (0, 0)))
    t = jnp.arange(S)
    win = jnp.array(POOL_WINDOWS, dtype=jnp.int32)
    lo = jnp.maximum(t[:, None] + 1 - win[None, :], 0)
    g_idx = jnp.arange(C_GROUPS)[None, :]
    total = cs[:, 1:] - cs[:, lo, g_idx]
    count = (t[:, None] + 1 - lo).astype(jnp.float32)
    pooled = total / count[None, :, :, None] - xf
    y = jnp.einsum('bsgc,gcd->bsgd', pooled, w_p.astype(jnp.float32))
    y = y * s_p.astype(jnp.float32).reshape(C_GROUPS, C_GD)
    return y.reshape(Bn, S, W_GROUP).astype(xc.dtype)


def stick_breaking(q, k, v):
    Bn, S, H, hd = q.shape
    scale = hd ** -0.5
    outs = []
    for i in range(S // SB_QBLOCK):
        q0, q1 = i * SB_QBLOCK, (i + 1) * SB_QBLOCK
        z = jnp.einsum('bqhd,bkhd->bhqk', q[:, q0:q1], k[:, :q1]).astype(jnp.float32) * scale
        past = jnp.arange(q1)[None, :] < jnp.arange(q0, q1)[:, None]
        log_fail = jnp.where(past, jax.nn.log_sigmoid(-z), 0.0)
        after = lax.cumsum(log_fail, axis=3, reverse=True) - log_fail
        a = jnp.where(past, jnp.exp(jax.nn.log_sigmoid(z) + after), 0.0)
        outs.append(jnp.einsum('bhqk,bkhd->bqhd', a.astype(v.dtype), v[:, :q1]))
    return jnp.concatenate(outs, axis=1)


def setup_inputs(seed: int = 0) -> dict:
    key = jax.random.key(seed)
    ks = jax.random.split(key, 24)
    f32 = jnp.float32
    nrm = lambda k, shape, s: jax.random.normal(k, shape, f32) * s
    dt = jnp.exp(jax.random.uniform(ks[8], (DEPTH, A_HEADS), f32) * (jnp.log(0.1) - jnp.log(0.001)) + jnp.log(0.001))
    return {
        "x": nrm(ks[0], (BATCH, SEQ, D_MODEL), 1.0),
        "c": nrm(ks[1], (BATCH, D_MODEL), 1.0),
        "w_mod": nrm(ks[2], (DEPTH, D_MODEL, N_MOD * D_MODEL), 0.5 * D_MODEL ** -0.5),
        "b_mod": nrm(ks[3], (DEPTH, N_MOD * D_MODEL), 0.01),
        "g_norm1": 1.0 + nrm(ks[4], (DEPTH, D_MODEL), 0.02),
        "g_norm2": 1.0 + nrm(ks[5], (DEPTH, D_MODEL), 0.02),
        "w_in": nrm(ks[6], (DEPTH, D_MODEL, P_IN), D_MODEL ** -0.5),
        "conv_a": nrm(ks[7], (DEPTH, CONV_W, 3 * W_GROUP), CONV_W ** -0.5),
        "a_log": jnp.log(jax.random.uniform(ks[9], (DEPTH, A_HEADS), f32, 1.0, 16.0)),
        "dt_bias": dt + jnp.log(-jnp.expm1(-dt)),
        "g_out_a": 1.0 + nrm(ks[10], (DEPTH, W_GROUP), 0.02),
        "g_ln_b": 1.0 + nrm(ks[11], (DEPTH, W_GROUP), 0.02),
        "w_sg": nrm(ks[12], (DEPTH, B_HEADS, SG_BLOCK, SG_BLOCK), 0.5 * SG_BLOCK ** -0.5),
        "b_sg": 1.0 + nrm(ks[13], (DEPTH, B_HEADS, SG_BLOCK), 0.01),
        "w_pool": nrm(ks[14], (DEPTH, C_GROUPS, C_GD, C_GD), C_GD ** -0.5),
        "s_pool": 1.0 + nrm(ks[15], (DEPTH, W_GROUP), 0.1),
        "g_out_d": 1.0 + nrm(ks[16], (DEPTH, W_GROUP), 0.02),
        "w_out": nrm(ks[17], (DEPTH, D_MIX, D_MODEL), D_MIX ** -0.5),
        "w_ff1": nrm(ks[18], (DEPTH, D_MODEL, D_FF), D_MODEL ** -0.5),
        "w_ff2": nrm(ks[19], (DEPTH, D_FF, D_MODEL), D_FF ** -0.5),
        "g_final": 1.0 + nrm(ks[20], (D_MODEL,), 0.02),
    }


def reference(x, c, w_mod, b_mod, g_norm1, g_norm2, w_in, conv_a, a_log, dt_bias, g_out_a, g_ln_b,
              w_sg, b_sg, w_pool, s_pool, g_out_d, w_out, w_ff1, w_ff2, g_final):
    Bn, S, _ = x.shape
    cond = jax.nn.silu(c)
    for l in range(DEPTH):
        mod = cond @ w_mod[l] + b_mod[l]
        shift1, scale1, gate1, shift2, scale2, gate2 = jnp.split(mod[:, None, :], N_MOD, axis=-1)
        h = rms_norm(x, g_norm1[l]) * (1.0 + scale1) + shift1
        p = h @ w_in[l]
        qkv_a, a_a, b_a, gate_a, u_b, v_b, x_c, qkv_d = split_cols(p, COL_SIZES)
        y_a = gated_deltanet(qkv_a, a_a, b_a, gate_a, conv_a[l], a_log[l], dt_bias[l], g_out_a[l])
        y_b = spatial_gating(u_b, v_b, g_ln_b[l], w_sg[l], b_sg[l])
        y_c = multiscale_pool(x_c, w_pool[l], s_pool[l])
        q_d, k_d, v_d = [t.reshape(Bn, S, D_HEADS, D_HD) for t in jnp.split(qkv_d, 3, axis=-1)]
        y_d = rms_norm(stick_breaking(q_d, k_d, v_d), g_out_d[l].reshape(D_HEADS, D_HD)).reshape(Bn, S, W_GROUP)
        y = jnp.concatenate([y_a, y_b, y_c, y_d], axis=-1) @ w_out[l]
        x = x + gate1 * y
        h = rms_norm(x, g_norm2[l]) * (1.0 + scale2) + shift2
        x = x + gate2 * (jnp.square(jax.nn.relu(h @ w_ff1[l])) @ w_ff2[l])
    return rms_norm(x, g_final)
```

```python
import functools

import jax
import jax.numpy as jnp
from jax import lax
from jax.experimental import pallas as pl
from jax.experimental.pallas import tpu as pltpu

F32 = jnp.float32
BF16 = jnp.bfloat16

EPS = 1e-6
N_MOD = 6
HEADS = 4
HD = 64
WG = HEADS * HD
CHUNK = 64
CONV_W = 4
SG_BLOCK = 128
SB_QBLOCK = 128
POOL_WINDOWS = (2, 4, 8, 16)
POOL_HALO = 16
PREP_ROWS = 256
SB_ZERO_LOG = -110.0

COL_QKV_A, COL_QKV_D, COL_GATE_A, COL_U_B, COL_V_B, COL_X_C, COL_AB = 0, 768, 1536, 1792, 2048, 2304, 2560
AB_PAD = 128
P_PAD = COL_AB + AB_PAD

VMEM_LIMIT = 56 * 1024 * 1024


def _dot(a, b):
    return jnp.dot(a, b, preferred_element_type=F32)


def _dot_nt(a, b):
    return lax.dot_general(a, b, (((1,), (1,)), ((), ())), preferred_element_type=F32)


def _dot_tn(a, b):
    return lax.dot_general(a, b, (((0,), (0,)), ((), ())), preferred_element_type=F32)


def _split(x):
    hi = x.astype(BF16)
    lo = (x - hi.astype(F32)).astype(BF16)
    return hi, lo


def _dot_exact_rhs(x, m):
    hi, lo = _split(x)
    return _dot(hi, m) + _dot(lo, m)


def _dot_exact_lhs(m, x):
    hi, lo = _split(x)
    return _dot(m, hi) + _dot(m, lo)


def _sigmoid(x):
    return 1.0 / (1.0 + jnp.exp(-x))


def _softplus(x):
    return jnp.maximum(x, 0.0) + jnp.log(1.0 + jnp.exp(-jnp.abs(x)))


def _lane_head(shape):
    return lax.broadcasted_iota(jnp.int32, shape, 1) // HD


def _blockdiag(y, width):
    lane_blk = lax.broadcasted_iota(jnp.int32, y.shape, 1) // width
    return jnp.concatenate([jnp.where(lane_blk == h, y, 0.0) for h in range(HEADS)], axis=0)


def _shift_rows(x, k):
    return pltpu.roll(x, k, axis=0)


def _mod_kernel(c_ref, w_ref, b_ref, o_ref):
    c = c_ref[...]
    cond = c * _sigmoid(c)
    o_ref[0] = _dot(cond.astype(BF16), w_ref[0].astype(BF16)) + b_ref[0]


def _modulation(c, w_mod, b_mod):
    depth, d, n6 = w_mod.shape
    b = c.shape[0]
    nblk = n6 // d
    return pl.pallas_call(
        _mod_kernel,
        grid=(depth, nblk),
        in_specs=[
            pl.BlockSpec((b, d), lambda l, j: (0, 0)),
            pl.BlockSpec((1, d, d), lambda l, j: (l, 0, j)),
            pl.BlockSpec((1, 1, d), lambda l, j: (l, 0, j)),
        ],
        out_specs=pl.BlockSpec((1, b, d), lambda l, j: (l, 0, j)),
        out_shape=jax.ShapeDtypeStruct((depth, b, n6), F32),
        name="modulation",
    )(c, w_mod, b_mod.reshape(depth, 1, n6))


def _rms(x, g):
    return x * lax.rsqrt(jnp.mean(x * x, axis=-1, keepdims=True) + EPS) * g


def _inproj_kernel(x_ref, mod_ref, g_ref, w_ref, o_ref):
    m = mod_ref[0]
    h = _rms(x_ref[...], g_ref[...]) * (1.0 + m[1:2]) + m[0:1]
    o_ref[...] = _dot(h.astype(BF16), w_ref[...])


def _inproj(x2, mod_l, g1, w_in_p, seq, tm):
    n, d = x2.shape
    tiles_per_seq = seq // tm
    return pl.pallas_call(
        _inproj_kernel,
        grid=(n // tm,),
        in_specs=[
            pl.BlockSpec((tm, d), lambda i: (i, 0)),
            pl.BlockSpec((1, N_MOD, d), lambda i: (i // tiles_per_seq, 0, 0)),
            pl.BlockSpec((1, d), lambda i: (0, 0)),
            pl.BlockSpec((d, P_PAD), lambda i: (0, 0)),
        ],
        out_specs=pl.BlockSpec((tm, P_PAD), lambda i: (i, 0)),
        out_shape=jax.ShapeDtypeStruct((n, P_PAD), F32),
        compiler_params=pltpu.CompilerParams(vmem_limit_bytes=VMEM_LIMIT),
        name="inproj",
    )(x2, mod_l, g1, w_in_p)


def _gdn_kernel(qkv_ref, ab_ref, gate_ref, convw_ref, alog_ref, dtb_ref, gout_ref,
                gsum_ref, expand_ref, ltri_ref, o_ref,
                q_s, k_s, v_s, beta_s, gc_s, o_s, state_s):
    seq = qkv_ref.shape[0]
    gsum = gsum_ref[...]

    def prep(r, _):
        r0 = pl.multiple_of(r * PREP_ROWS, PREP_ROWS)
        cur = qkv_ref[pl.ds(r0, PREP_ROWS), :]
        prev = qkv_ref[pl.ds(pl.multiple_of(jnp.maximum(r0 - 8, 0), 8), 8), :]
        prev = jnp.where(r > 0, prev, 0.0)
        ext = jnp.concatenate([prev, cur], axis=0)
        acc = cur * convw_ref[CONV_W - 1:CONV_W, :]
        for k in range(1, CONV_W):
            acc = acc + _shift_rows(ext, k)[8:, :] * convw_ref[CONV_W - 1 - k:CONV_W - k, :]
        y = acc * _sigmoid(acc)
        q, kk, v = y[:, :WG], y[:, WG:2 * WG], y[:, 2 * WG:]
        q = q * lax.rsqrt(_dot_exact_rhs(q * q, gsum) + EPS) * (HD ** -0.5)
        kk = kk * lax.rsqrt(_dot_exact_rhs(kk * kk, gsum) + EPS)
        abx = _dot_exact_rhs(ab_ref[pl.ds(r0, PREP_ROWS), :], expand_ref[...])
        a_b, b_b = abx[:, :WG], abx[:, WG:]
        g = -jnp.exp(alog_ref[...]) * _softplus(a_b + dtb_ref[...])
        rows = pl.ds(r0, PREP_ROWS)
        q_s[rows, :] = q
        k_s[rows, :] = kk
        v_s[rows, :] = v
        beta_s[rows, :] = _sigmoid(b_b)
        gc_s[rows, :] = _dot_exact_lhs(ltri_ref[...], g)
        return 0

    lax.fori_loop(0, seq // PREP_ROWS, prep, 0)

    state_s[...] = jnp.zeros_like(state_s)
    shape = (CHUNK, WG)
    row = lax.broadcasted_iota(jnp.int32, shape, 0)
    col = lax.broadcasted_iota(jnp.int32, shape, 1) % CHUNK
    eye = row == col
    incl = row >= col
    strict = row > col
    state_blk = (lax.broadcasted_iota(jnp.int32, (WG, WG), 0) // HD) == _lane_head((WG, WG))

    def bd(y):
        return _blockdiag(y, HD).astype(BF16)

    def chunk(c, _):
        rows = pl.ds(pl.multiple_of(c * CHUNK, CHUNK), CHUNK)
        qn, kn, v, beta, gc = q_s[rows, :], k_s[rows, :], v_s[rows, :], beta_s[rows, :], gc_s[rows, :]
        eg = jnp.exp(gc)
        gl = gc[CHUNK - 1:CHUNK, :]
        kb = kn * beta
        kbd = bd(kn)
        a_kk = _dot_nt(kb.astype(BF16), kbd)
        a_qk = _dot_nt(qn.astype(BF16), kbd)
        gc_row = jnp.sum(jnp.where(eye, gc, 0.0), axis=0, keepdims=True)
        decay = jnp.where(incl, jnp.exp(gc - gc_row), 0.0)
        low = jnp.where(strict, a_kk * decay, 0.0)
        qk = a_qk * decay
        t = jnp.where(eye, 1.0, 0.0) - jnp.where(row // 2 == col // 2, low, 0.0)
        s = 2
        while s < CHUNK:
            join = jnp.logical_and(row // (2 * s) == col // (2 * s), row // s != col // s)
            te = _dot(t.astype(BF16), bd(jnp.where(join, low, 0.0)))
            t = t - _dot(te.astype(BF16), bd(t))
            s *= 2
        tb = t.astype(BF16)
        u = _dot(tb, bd(v * beta))
        w = _dot(tb, bd(kb * eg))
        state = state_s[...]
        wq = _dot(jnp.concatenate([w, qn * eg], axis=0).astype(BF16), state.astype(BF16))
        v_new = u - wq[:CHUNK]
        o_s[rows, :] = wq[CHUNK:] + _dot(qk.astype(BF16), bd(v_new))
        kd = kn * jnp.exp(gl - gc)
        upd = _dot_tn(kd.astype(BF16), v_new.astype(BF16))
        state_s[...] = state * jnp.exp(gl) + jnp.where(state_blk, upd, 0.0)
        return 0

    lax.fori_loop(0, seq // CHUNK, chunk, 0)

    def post(r, _):
        rows = pl.ds(pl.multiple_of(r * PREP_ROWS, PREP_ROWS), PREP_ROWS)
        o = o_s[rows, :]
        ms = _dot_exact_rhs(o * o, gsum) * (1.0 / HD)
        gate = gate_ref[rows, :]
        o_ref[rows, :] = o * lax.rsqrt(ms + EPS) * gout_ref[...] * (gate * _sigmoid(gate))
        return 0

    lax.fori_loop(0, seq // PREP_ROWS, post, 0)


def _gdn(p, conv_w, alog_b, dtb_b, gout, consts, batch, seq):
    n = batch * seq
    vec = lambda w: pl.BlockSpec((1, w), lambda b: (0, 0))
    full = lambda a: pl.BlockSpec(a.shape, lambda b: (0,) * a.ndim)
    return pl.pallas_call(
        _gdn_kernel,
        grid=(batch,),
        in_specs=[
            pl.BlockSpec((seq, 3 * WG), lambda b: (b, COL_QKV_A // (3 * WG))),
            pl.BlockSpec((seq, AB_PAD), lambda b: (b, COL_AB // AB_PAD)),
            pl.BlockSpec((seq, WG), lambda b: (b, COL_GATE_A // WG)),
            full(conv_w), vec(WG), vec(WG), vec(WG),
            full(consts["gsum"]), full(consts["expand"]), full(consts["ltri"]),
        ],
        out_specs=pl.BlockSpec((seq, WG), lambda b: (b, 0)),
        out_shape=jax.ShapeDtypeStruct((n, WG), F32),
        scratch_shapes=[pltpu.VMEM((seq, WG), F32)] * 6 + [pltpu.VMEM((WG, WG), F32)],
        compiler_params=pltpu.CompilerParams(vmem_limit_bytes=VMEM_LIMIT),
        name="gated_deltanet",
    )(p, p, p, conv_w, alog_b, dtb_b, gout, consts["gsum"], consts["expand"], consts["ltri"])


def _sgpool_kernel(u_ref, v_ref, x_ref, gln_ref, wsg_ref, bsg_ref, wpool_ref, spool_ref, gsum_ref,
                   yb_ref, yc_ref):
    seq = u_ref.shape[0]
    gsum = gsum_ref[...]
    t_chunk = lax.broadcasted_iota(jnp.int32, wsg_ref.shape, 0) // CHUNK
    s_chunk = (lax.broadcasted_iota(jnp.int32, wsg_ref.shape, 1) % SG_BLOCK) // CHUNK
    wm = jnp.where(t_chunk >= s_chunk, wsg_ref[...], 0.0).astype(BF16)

    def gate_block(n, _):
        rows = pl.ds(pl.multiple_of(n * SG_BLOCK, SG_BLOCK), SG_BLOCK)
        v = v_ref[rows, :]
        mu = _dot_exact_rhs(v, gsum) * (1.0 / HD)
        vc = v - mu
        var = _dot_exact_rhs(vc * vc, gsum) * (1.0 / HD)
        vn = vc * lax.rsqrt(var + EPS) * gln_ref[...]
        s = _dot(wm, _blockdiag(vn, HD).astype(BF16)) + bsg_ref[...]
        yb_ref[rows, :] = u_ref[rows, :] * s
        return 0

    lax.fori_loop(0, seq // SG_BLOCK, gate_block, 0)

    lane_grp = _lane_head((PREP_ROWS, WG))
    win = jnp.zeros((PREP_ROWS, WG), jnp.int32)
    for g, w in enumerate(POOL_WINDOWS):
        win = jnp.where(lane_grp == g, w, win)

    def pool_block(r, _):
        r0 = pl.multiple_of(r * PREP_ROWS, PREP_ROWS)
        cur = x_ref[pl.ds(r0, PREP_ROWS), :]
        prev = x_ref[pl.ds(pl.multiple_of(jnp.maximum(r0 - POOL_HALO, 0), POOL_HALO), POOL_HALO), :]
        prev = jnp.where(r > 0, prev, 0.0)
        s = jnp.concatenate([prev, cur], axis=0)
        total = jnp.zeros((PREP_ROWS, WG), F32)
        k = 1
        for g, w in enumerate(POOL_WINDOWS):
            while k < w:
                s = s + _shift_rows(s, k)
                k *= 2
            total = jnp.where(lane_grp == g, s[POOL_HALO:, :], total)
        t = r0 + lax.broadcasted_iota(jnp.int32, (PREP_ROWS, WG), 0)
        count = jnp.minimum(t + 1, win).astype(F32)
        pooled = total / count - cur
        yc_ref[pl.ds(r0, PREP_ROWS), :] = _dot(pooled.astype(BF16), wpool_ref[...]) * spool_ref[...]
        return 0

    lax.fori_loop(0, seq // PREP_ROWS, pool_block, 0)


def _sgpool(p, gln, wsg_cat, bsg_packed, wpool_bd, spool, consts, batch, seq):
    n = batch * seq
    vec = lambda w: pl.BlockSpec((1, w), lambda b: (0, 0))
    full = lambda a: pl.BlockSpec(a.shape, lambda b: (0,) * a.ndim)
    col = lambda c: pl.BlockSpec((seq, WG), lambda b: (b, c // WG))
    out = pl.BlockSpec((seq, WG), lambda b: (b, 0))
    return pl.pallas_call(
        _sgpool_kernel,
        grid=(batch,),
        in_specs=[col(COL_U_B), col(COL_V_B), col(COL_X_C), vec(WG), full(wsg_cat), full(bsg_packed),
                  full(wpool_bd), vec(WG), full(consts["gsum"])],
        out_specs=[out, out],
        out_shape=[jax.ShapeDtypeStruct((n, WG), F32)] * 2,
        compiler_params=pltpu.CompilerParams(vmem_limit_bytes=VMEM_LIMIT),
        name="spatial_gating_pool",
    )(p, p, p, gln, wsg_cat, bsg_packed, wpool_bd, spool, consts["gsum"])


def _sb_kernel(qkv_ref, gout_ref, gsum_ref, sufsum_ref, o_ref, kbd_s, vbd_s, acc_s, carry_s):
    seq = qkv_ref.shape[0]
    qi = pl.program_id(1)
    blk = SB_QBLOCK

    @pl.when(qi == 0)
    def _():
        def build(j, _):
            rows = pl.ds(pl.multiple_of(j * blk, blk), blk)
            kbd_s[j] = _blockdiag(qkv_ref[rows, WG:2 * WG], HD).astype(BF16)
            vbd_s[j] = _blockdiag(qkv_ref[rows, 2 * WG:], HD).astype(BF16)
            return 0
        lax.fori_loop(0, seq // blk, build, 0)

    q = (qkv_ref[pl.ds(pl.multiple_of(qi * blk, blk), blk), :WG] * (HD ** -0.5)).astype(BF16)
    sufsum = sufsum_ref[...]
    past = lax.broadcasted_iota(jnp.int32, (blk, blk), 1) < lax.broadcasted_iota(jnp.int32, (blk, blk), 0)

    def block(j, diagonal):
        z = _dot_nt(q, kbd_s[j])
        probs = []
        for h in range(HEADS):
            zh = z[:, h * blk:(h + 1) * blk]
            lf = -_softplus(zh)
            if diagonal:
                lf = jnp.where(past, lf, 0.0)
            cs = _dot_exact_rhs(lf, sufsum)
            carry = carry_s[:, h * blk:(h + 1) * blk]
            a = jnp.exp(lf + zh + cs[:, :blk] + carry)
            if diagonal:
                a = jnp.where(past, a, 0.0)
            carry_s[:, h * blk:(h + 1) * blk] = carry + cs[:, blk:]
            probs.append(a.astype(BF16))
        acc_s[...] += _dot(jnp.concatenate(probs, axis=1), vbd_s[j])

    acc_s[...] = jnp.zeros_like(acc_s)
    carry_s[...] = jnp.zeros_like(carry_s)
    block(qi, True)

    def cond(state):
        j, largest_carry = state
        return jnp.logical_and(j >= 0, largest_carry > SB_ZERO_LOG)

    def body(state):
        j, _ = state
        block(j, False)
        return j - 1, jnp.max(carry_s[...])

    lax.while_loop(cond, body, (qi - 1, jnp.max(carry_s[...])))

    o = acc_s[...]
    ms = _dot_exact_rhs(o * o, gsum_ref[...]) * (1.0 / HD)
    o_ref[...] = o * lax.rsqrt(ms + EPS) * gout_ref[...]


def _stick_breaking(p, gout, consts, batch, seq):
    n = batch * seq
    nq = seq // SB_QBLOCK
    full = lambda a: pl.BlockSpec(a.shape, lambda b, i: (0,) * a.ndim)
    return pl.pallas_call(
        _sb_kernel,
        grid=(batch, nq),
        in_specs=[
            pl.BlockSpec((seq, 3 * WG), lambda b, i: (b, COL_QKV_D // (3 * WG))),
            pl.BlockSpec((1, WG), lambda b, i: (0, 0)),
            full(consts["gsum"]), full(consts["sufsum"]),
        ],
        out_specs=pl.BlockSpec((SB_QBLOCK, WG), lambda b, i: (b * nq + i, 0)),
        out_shape=jax.ShapeDtypeStruct((n, WG), F32),
        scratch_shapes=[
            pltpu.VMEM((nq, HEADS * SB_QBLOCK, WG), BF16),
            pltpu.VMEM((nq, HEADS * SB_QBLOCK, WG), BF16),
            pltpu.VMEM((SB_QBLOCK, WG), F32),
            pltpu.VMEM((SB_QBLOCK, HEADS * SB_QBLOCK), F32),
        ],
        compiler_params=pltpu.CompilerParams(
            dimension_semantics=("arbitrary", "arbitrary"), vmem_limit_bytes=VMEM_LIMIT),
        name="stick_breaking",
    )(p, gout, consts["gsum"], consts["sufsum"])


def _outmlp_kernel(x_ref, ya_ref, yb_ref, yc_ref, yd_ref, mod_ref, g2_ref, gfin_ref,
                   wout_ref, w1_ref, w2_ref, o_ref, *, ff_chunk, final):
    m = mod_ref[0]
    y = jnp.concatenate([r[...].astype(BF16) for r in (ya_ref, yb_ref, yc_ref, yd_ref)], axis=1)
    x = x_ref[...] + m[2:3] * _dot(y, wout_ref[...])
    h = (_rms(x, g2_ref[...]) * (1.0 + m[4:5]) + m[3:4]).astype(BF16)
    d_ff = w1_ref.shape[1]
    ff = jnp.zeros(x.shape, F32)
    for k in range(0, d_ff, ff_chunk):
        a = jnp.maximum(_dot(h, w1_ref[:, k:k + ff_chunk]), 0.0)
        ff = ff + _dot((a * a).astype(BF16), w2_ref[k:k + ff_chunk, :])
    x = x + m[5:6] * ff
    if final:
        x = _rms(x, gfin_ref[...])
    o_ref[...] = x


def _outmlp(x2, ys, mod_l, g2, gfin, w_out, w_ff1, w_ff2, seq, tm, final):
    n, d = x2.shape
    d_ff = w_ff1.shape[1]
    tiles_per_seq = seq // tm
    tile = lambda w: pl.BlockSpec((tm, w), lambda i: (i, 0))
    const = lambda a: pl.BlockSpec(a.shape, lambda i: (0,) * a.ndim, pipeline_mode=pl.Buffered(1))
    return pl.pallas_call(
        functools.partial(_outmlp_kernel, ff_chunk=min(d_ff, 1024), final=final),
        grid=(n // tm,),
        in_specs=[
            tile(d), tile(WG), tile(WG), tile(WG), tile(WG),
            pl.BlockSpec((1, N_MOD, d), lambda i: (i // tiles_per_seq, 0, 0)),
            pl.BlockSpec((1, d), lambda i: (0, 0)),
            pl.BlockSpec((1, d), lambda i: (0, 0)),
            const(w_out), const(w_ff1), const(w_ff2),
        ],
        out_specs=tile(d),
        out_shape=jax.ShapeDtypeStruct((n, d), F32),
        compiler_params=pltpu.CompilerParams(vmem_limit_bytes=VMEM_LIMIT),
        name="outproj_mlp",
    )(x2, *ys, mod_l, g2, gfin, w_out, w_ff1, w_ff2)


def _constants():
    idx = jnp.arange(WG)
    gsum = (idx[:, None] // HD == idx[None, :] // HD).astype(BF16)
    ab = jnp.arange(AB_PAD)[:, None]
    lane = jnp.arange(2 * WG)[None, :]
    expand = (ab == (lane // WG) * HEADS + (lane % WG) // HD).astype(BF16)
    r = jnp.arange(PREP_ROWS)
    ltri = ((r[:, None] // CHUNK == r[None, :] // CHUNK) & (r[:, None] >= r[None, :])).astype(BF16)
    k = jnp.arange(SB_QBLOCK)
    suf = (k[:, None] > k[None, :]).astype(BF16)
    sufsum = jnp.concatenate([suf, jnp.ones((SB_QBLOCK, SB_QBLOCK), BF16)], axis=1)
    return {"gsum": gsum, "expand": expand, "ltri": ltri, "sufsum": sufsum}


def _pad_w_in(w_in):
    o = 0
    parts = {}
    for name, size in (("qkv_a", 3 * WG), ("ab", 2 * HEADS), ("gate_a", WG), ("u_b", WG), ("v_b", WG),
                       ("x_c", WG), ("qkv_d", 3 * WG)):
        parts[name] = w_in[..., o:o + size]
        o += size
    ab = jnp.pad(parts["ab"], ((0, 0), (0, 0), (0, AB_PAD - 2 * HEADS)))
    return jnp.concatenate([parts["qkv_a"], parts["qkv_d"], parts["gate_a"], parts["u_b"], parts["v_b"],
                            parts["x_c"], ab], axis=-1).astype(BF16)


def _per_head_lanes(v):
    return jnp.repeat(v, HD, axis=-1)[..., None, :]


def kernel(x, c, w_mod, b_mod, g_norm1, g_norm2, w_in, conv_a, a_log, dt_bias, g_out_a, g_ln_b, w_sg, b_sg,
           w_pool, s_pool, g_out_d, w_out, w_ff1, w_ff2, g_final):
    batch, seq, d = x.shape
    depth = w_mod.shape[0]
    assert seq % PREP_ROWS == 0 and d == HEADS * WG
    tm = min(512, seq)
    consts = _constants()

    mod = _modulation(c, w_mod, b_mod).reshape(depth, batch, N_MOD, d)
    w_in_p = _pad_w_in(w_in)
    w_out_b, w_ff1_b, w_ff2_b = w_out.astype(BF16), w_ff1.astype(BF16), w_ff2.astype(BF16)
    alog_b, dtb_b = _per_head_lanes(a_log), _per_head_lanes(dt_bias)
    wsg_cat = jnp.transpose(w_sg, (0, 2, 1, 3)).reshape(depth, SG_BLOCK, HEADS * SG_BLOCK)
    bsg_packed = jnp.repeat(jnp.transpose(b_sg, (0, 2, 1)), HD, axis=-1)
    wpool_bd = (jnp.eye(HEADS, dtype=F32)[None, :, None, :, None] * w_pool[:, :, :, None, :]).reshape(
        depth, WG, WG).astype(BF16)
    row = lambda v: v[:, None, :]

    x2 = x.reshape(batch * seq, d)
    for l in range(depth):
        p = _inproj(x2, mod[l], row(g_norm1)[l], w_in_p[l], seq, tm)
        y_a = _gdn(p, conv_a[l], alog_b[l], dtb_b[l], row(g_out_a)[l], consts, batch, seq)
        y_b, y_c = _sgpool(p, row(g_ln_b)[l], wsg_cat[l], bsg_packed[l], wpool_bd[l], row(s_pool)[l], consts,
                           batch, seq)
        y_d = _stick_breaking(p, row(g_out_d)[l], consts, batch, seq)
        x2 = _outmlp(x2, (y_a, y_b, y_c, y_d), mod[l], row(g_norm2)[l], g_final[None, :], w_out_b[l],
                     w_ff1_b[l], w_ff2_b[l], seq, tm, final=(l == depth - 1))
    return x2.reshape(batch, seq, d)
```

```python
import functools

import jax
import jax.numpy as jnp
from jax import lax
from jax.experimental import pallas as pl
from jax.experimental.pallas import tpu as pltpu

F32 = jnp.float32
BF16 = jnp.bfloat16

EPS = 1e-6
N_MOD = 6
HEADS = 4
HD = 64
WG = HEADS * HD
CHUNK = 64
CONV_W = 4
SG_BLOCK = 128
SB_QBLOCK = 128
POOL_WINDOWS = (2, 4, 8, 16)
POOL_HALO = 16
PREP_ROWS = 256
GDN_INTERLEAVE = 8
SB_ZERO_LOG = -110.0

COL_QKV_A, COL_QKV_D, COL_GATE_A, COL_U_B, COL_V_B, COL_X_C, COL_AB = 0, 768, 1536, 1792, 2048, 2304, 2560
AB_PAD = 128
P_PAD = COL_AB + AB_PAD

VMEM_LIMIT = 56 * 1024 * 1024


def _dot(a, b):
    return jnp.dot(a, b, preferred_element_type=F32)


def _dot_nt(a, b):
    return lax.dot_general(a, b, (((1,), (1,)), ((), ())), preferred_element_type=F32)


def _dot_tn(a, b):
    return lax.dot_general(a, b, (((0,), (0,)), ((), ())), preferred_element_type=F32)


def _split(x):
    hi = x.astype(BF16)
    lo = (x - hi.astype(F32)).astype(BF16)
    return hi, lo


def _dot_exact_rhs(x, m):
    hi, lo = _split(x)
    return _dot(hi, m) + _dot(lo, m)


def _dot_exact_lhs(m, x):
    hi, lo = _split(x)
    return _dot(m, hi) + _dot(m, lo)


def _sigmoid(x):
    return 1.0 / (1.0 + jnp.exp(-x))


def _softplus(x):
    return jnp.maximum(x, 0.0) + jnp.log(1.0 + jnp.exp(-jnp.abs(x)))


def _lane_head(shape):
    return lax.broadcasted_iota(jnp.int32, shape, 1) // HD


def _blockdiag(y, width):
    lane_blk = lax.broadcasted_iota(jnp.int32, y.shape, 1) // width
    return jnp.concatenate([jnp.where(lane_blk == h, y, 0.0) for h in range(HEADS)], axis=0)


def _shift_rows(x, k):
    return pltpu.roll(x, k, axis=0)


def _mod_kernel(c_ref, w_ref, b_ref, o_ref):
    c = c_ref[...]
    cond = c * _sigmoid(c)
    o_ref[0] = _dot(cond.astype(BF16), w_ref[0].astype(BF16)) + b_ref[0]


def _modulation(c, w_mod, b_mod):
    depth, d, n6 = w_mod.shape
    b = c.shape[0]
    nblk = n6 // d
    return pl.pallas_call(
        _mod_kernel,
        grid=(depth, nblk),
        in_specs=[
            pl.BlockSpec((b, d), lambda l, j: (0, 0)),
            pl.BlockSpec((1, d, d), lambda l, j: (l, 0, j)),
            pl.BlockSpec((1, 1, d), lambda l, j: (l, 0, j)),
        ],
        out_specs=pl.BlockSpec((1, b, d), lambda l, j: (l, 0, j)),
        out_shape=jax.ShapeDtypeStruct((depth, b, n6), F32),
        name="modulation",
    )(c, w_mod, b_mod.reshape(depth, 1, n6))


def _rms(x, g):
    return x * lax.rsqrt(jnp.mean(x * x, axis=-1, keepdims=True) + EPS) * g


def _inproj_kernel(x_ref, mod_ref, g_ref, w_ref, o_ref):
    m = mod_ref[0]
    h = _rms(x_ref[...], g_ref[...]) * (1.0 + m[1:2]) + m[0:1]
    o_ref[...] = _dot(h.astype(BF16), w_ref[...])


def _inproj(x2, mod_l, g1, w_in_p, seq, tm):
    n, d = x2.shape
    tiles_per_seq = seq // tm
    return pl.pallas_call(
        _inproj_kernel,
        grid=(n // tm,),
        in_specs=[
            pl.BlockSpec((tm, d), lambda i: (i, 0)),
            pl.BlockSpec((1, N_MOD, d), lambda i: (i // tiles_per_seq, 0, 0)),
            pl.BlockSpec((1, d), lambda i: (0, 0)),
            pl.BlockSpec((d, P_PAD), lambda i: (0, 0)),
        ],
        out_specs=pl.BlockSpec((tm, P_PAD), lambda i: (i, 0)),
        out_shape=jax.ShapeDtypeStruct((n, P_PAD), F32),
        compiler_params=pltpu.CompilerParams(vmem_limit_bytes=VMEM_LIMIT),
        name="inproj",
    )(x2, mod_l, g1, w_in_p)


def _gdn_kernel(qkv_ref, ab_ref, gate_ref, convw_ref, alog_ref, dtb_ref, gout_ref,
                gsum_ref, expand_ref, ltri_ref, o_ref,
                q_s, k_s, v_s, beta_s, gc_s, o_s, u_s, wq_s, qk_s, kd_s, state_s):
    seq = qkv_ref.shape[0]
    gsum = gsum_ref[...]

    def prep(r, _):
        r0 = pl.multiple_of(r * PREP_ROWS, PREP_ROWS)
        cur = qkv_ref[pl.ds(r0, PREP_ROWS), :]
        prev = qkv_ref[pl.ds(pl.multiple_of(jnp.maximum(r0 - 8, 0), 8), 8), :]
        prev = jnp.where(r > 0, prev, 0.0)
        ext = jnp.concatenate([prev, cur], axis=0)
        acc = cur * convw_ref[CONV_W - 1:CONV_W, :]
        for k in range(1, CONV_W):
            acc = acc + _shift_rows(ext, k)[8:, :] * convw_ref[CONV_W - 1 - k:CONV_W - k, :]
        y = acc * _sigmoid(acc)
        q, kk, v = y[:, :WG], y[:, WG:2 * WG], y[:, 2 * WG:]
        q = q * lax.rsqrt(_dot_exact_rhs(q * q, gsum) + EPS) * (HD ** -0.5)
        kk = kk * lax.rsqrt(_dot_exact_rhs(kk * kk, gsum) + EPS)
        abx = _dot_exact_rhs(ab_ref[pl.ds(r0, PREP_ROWS), :], expand_ref[...])
        a_b, b_b = abx[:, :WG], abx[:, WG:]
        g = -jnp.exp(alog_ref[...]) * _softplus(a_b + dtb_ref[...])
        rows = pl.ds(r0, PREP_ROWS)
        q_s[rows, :] = q
        k_s[rows, :] = kk
        v_s[rows, :] = v
        beta_s[rows, :] = _sigmoid(b_b)
        gc_s[rows, :] = _dot_exact_lhs(ltri_ref[...], g)
        return 0

    lax.fori_loop(0, seq // PREP_ROWS, prep, 0)

    state_s[...] = jnp.zeros_like(state_s)
    shape = (CHUNK, WG)
    row = lax.broadcasted_iota(jnp.int32, shape, 0)
    col = lax.broadcasted_iota(jnp.int32, shape, 1) % CHUNK
    eye = row == col
    incl = row >= col
    strict = row > col
    state_blk = (lax.broadcasted_iota(jnp.int32, (WG, WG), 0) // HD) == _lane_head((WG, WG))

    def bd(y):
        return _blockdiag(y, HD).astype(BF16)

    def intra_group(i, _):
        chunks = [i * GDN_INTERLEAVE + k for k in range(GDN_INTERLEAVE)]
        rows = [pl.ds(pl.multiple_of(c * CHUNK, CHUNK), CHUNK) for c in chunks]
        lows, ts = [], []
        for c, r in zip(chunks, rows):
            qn, kn, gc = q_s[r, :], k_s[r, :], gc_s[r, :]
            kb = kn * beta_s[r, :]
            a = _dot_nt(jnp.concatenate([kb, qn], axis=0).astype(BF16), bd(kn))
            gc_row = jnp.sum(jnp.where(eye, gc, 0.0), axis=0, keepdims=True)
            decay = jnp.where(incl, jnp.exp(gc - gc_row), 0.0)
            low = jnp.where(strict, a[:CHUNK] * decay, 0.0)
            qk_s[r, :] = (a[CHUNK:] * decay).astype(BF16)
            wq_s[pl.ds(pl.multiple_of(c * 2 * CHUNK + CHUNK, CHUNK), CHUNK), :] = (qn * jnp.exp(gc)).astype(BF16)
            kd_s[r, :] = (kn * jnp.exp(gc[CHUNK - 1:CHUNK, :] - gc)).astype(BF16)
            lows.append(low)
            ts.append(jnp.where(eye, 1.0, 0.0) - jnp.where(row // 2 == col // 2, low, 0.0))
        s = 2
        while s < CHUNK:
            join = jnp.logical_and(row // (2 * s) == col // (2 * s), row // s != col // s)
            tes = [_dot(t.astype(BF16), bd(jnp.where(join, low, 0.0))) for t, low in zip(ts, lows)]
            ts = [t - _dot(te.astype(BF16), bd(t)) for t, te in zip(ts, tes)]
            s *= 2
        for c, r, t in zip(chunks, rows, ts):
            beta, gc = beta_s[r, :], gc_s[r, :]
            rhs = jnp.concatenate([bd(v_s[r, :] * beta), bd(k_s[r, :] * beta * jnp.exp(gc))], axis=1)
            uw = _dot(t.astype(BF16), rhs)
            u_s[r, :] = uw[:, :WG]
            wq_s[pl.ds(pl.multiple_of(c * 2 * CHUNK, 2 * CHUNK), CHUNK), :] = uw[:, WG:].astype(BF16)
        return 0

    lax.fori_loop(0, seq // (CHUNK * GDN_INTERLEAVE), intra_group, 0)

    def scan(c, _):
        rows = pl.ds(pl.multiple_of(c * CHUNK, CHUNK), CHUNK)
        state = state_s[...]
        wq = _dot(wq_s[pl.ds(pl.multiple_of(c * 2 * CHUNK, 2 * CHUNK), 2 * CHUNK), :], state.astype(BF16))
        v_new = u_s[rows, :] - wq[:CHUNK]
        o_s[rows, :] = wq[CHUNK:] + _dot(qk_s[rows, :], bd(v_new))
        upd = _dot_tn(kd_s[rows, :], v_new.astype(BF16))
        gl = gc_s[pl.ds(pl.multiple_of(c * CHUNK + CHUNK - 8, 8), 8), :][7:8, :]
        state_s[...] = state * jnp.exp(gl) + jnp.where(state_blk, upd, 0.0)
        return 0

    lax.fori_loop(0, seq // CHUNK, scan, 0)

    def post(r, _):
        rows = pl.ds(pl.multiple_of(r * PREP_ROWS, PREP_ROWS), PREP_ROWS)
        o = o_s[rows, :]
        ms = _dot_exact_rhs(o * o, gsum) * (1.0 / HD)
        gate = gate_ref[rows, :]
        o_ref[rows, :] = o * lax.rsqrt(ms + EPS) * gout_ref[...] * (gate * _sigmoid(gate))
        return 0

    lax.fori_loop(0, seq // PREP_ROWS, post, 0)


def _gdn(p, conv_w, alog_b, dtb_b, gout, consts, batch, seq):
    n = batch * seq
    vec = lambda w: pl.BlockSpec((1, w), lambda b: (0, 0))
    full = lambda a: pl.BlockSpec(a.shape, lambda b: (0,) * a.ndim)
    return pl.pallas_call(
        _gdn_kernel,
        grid=(batch,),
        in_specs=[
            pl.BlockSpec((seq, 3 * WG), lambda b: (b, COL_QKV_A // (3 * WG))),
            pl.BlockSpec((seq, AB_PAD), lambda b: (b, COL_AB // AB_PAD)),
            pl.BlockSpec((seq, WG), lambda b: (b, COL_GATE_A // WG)),
            full(conv_w), vec(WG), vec(WG), vec(WG),
            full(consts["gsum"]), full(consts["expand"]), full(consts["ltri"]),
        ],
        out_specs=pl.BlockSpec((seq, WG), lambda b: (b, 0)),
        out_shape=jax.ShapeDtypeStruct((n, WG), F32),
        scratch_shapes=[pltpu.VMEM((seq, WG), F32)] * 7 + [
            pltpu.VMEM((2 * seq, WG), BF16), pltpu.VMEM((seq, WG), BF16), pltpu.VMEM((seq, WG), BF16),
            pltpu.VMEM((WG, WG), F32)],
        compiler_params=pltpu.CompilerParams(vmem_limit_bytes=VMEM_LIMIT),
        name="gated_deltanet",
    )(p, p, p, conv_w, alog_b, dtb_b, gout, consts["gsum"], consts["expand"], consts["ltri"])


def _sgpool_kernel(u_ref, v_ref, x_ref, gln_ref, wsg_ref, bsg_ref, wpool_ref, spool_ref, gsum_ref,
                   yb_ref, yc_ref):
    seq = u_ref.shape[0]
    gsum = gsum_ref[...]
    t_chunk = lax.broadcasted_iota(jnp.int32, wsg_ref.shape, 0) // CHUNK
    s_chunk = (lax.broadcasted_iota(jnp.int32, wsg_ref.shape, 1) % SG_BLOCK) // CHUNK
    wm = jnp.where(t_chunk >= s_chunk, wsg_ref[...], 0.0).astype(BF16)

    def gate_block(n, _):
        rows = pl.ds(pl.multiple_of(n * SG_BLOCK, SG_BLOCK), SG_BLOCK)
        v = v_ref[rows, :]
        mu = _dot_exact_rhs(v, gsum) * (1.0 / HD)
        vc = v - mu
        var = _dot_exact_rhs(vc * vc, gsum) * (1.0 / HD)
        vn = vc * lax.rsqrt(var + EPS) * gln_ref[...]
        s = _dot(wm, _blockdiag(vn, HD).astype(BF16)) + bsg_ref[...]
        yb_ref[rows, :] = u_ref[rows, :] * s
        return 0

    lax.fori_loop(0, seq // SG_BLOCK, gate_block, 0)

    lane_grp = _lane_head((PREP_ROWS, WG))
    win = jnp.zeros((PREP_ROWS, WG), jnp.int32)
    for g, w in enumerate(POOL_WINDOWS):
        win = jnp.where(lane_grp == g, w, win)

    def pool_block(r, _):
        r0 = pl.multiple_of(r * PREP_ROWS, PREP_ROWS)
        cur = x_ref[pl.ds(r0, PREP_ROWS), :]
        prev = x_ref[pl.ds(pl.multiple_of(jnp.maximum(r0 - POOL_HALO, 0), POOL_HALO), POOL_HALO), :]
        prev = jnp.where(r > 0, prev, 0.0)
        s = jnp.concatenate([prev, cur], axis=0)
        total = jnp.zeros((PREP_ROWS, WG), F32)
        k = 1
        for g, w in enumerate(POOL_WINDOWS):
            while k < w:
                s = s + _shift_rows(s, k)
                k *= 2
            total = jnp.where(lane_grp == g, s[POOL_HALO:, :], total)
        t = r0 + lax.broadcasted_iota(jnp.int32, (PREP_ROWS, WG), 0)
        count = jnp.minimum(t + 1, win).astype(F32)
        pooled = total / count - cur
        yc_ref[pl.ds(r0, PREP_ROWS), :] = _dot(pooled.astype(BF16), wpool_ref[...]) * spool_ref[...]
        return 0

    lax.fori_loop(0, seq // PREP_ROWS, pool_block, 0)


def _sgpool(p, gln, wsg_cat, bsg_packed, wpool_bd, spool, consts, batch, seq):
    n = batch * seq
    vec = lambda w: pl.BlockSpec((1, w), lambda b: (0, 0))
    full = lambda a: pl.BlockSpec(a.shape, lambda b: (0,) * a.ndim)
    col = lambda c: pl.BlockSpec((seq, WG), lambda b: (b, c // WG))
    out = pl.BlockSpec((seq, WG), lambda b: (b, 0))
    return pl.pallas_call(
        _sgpool_kernel,
        grid=(batch,),
        in_specs=[col(COL_U_B), col(COL_V_B), col(COL_X_C), vec(WG), full(wsg_cat), full(bsg_packed),
                  full(wpool_bd), vec(WG), full(consts["gsum"])],
        out_specs=[out, out],
        out_shape=[jax.ShapeDtypeStruct((n, WG), F32)] * 2,
        compiler_params=pltpu.CompilerParams(vmem_limit_bytes=VMEM_LIMIT),
        name="spatial_gating_pool",
    )(p, p, p, gln, wsg_cat, bsg_packed, wpool_bd, spool, consts["gsum"])


def _sb_kernel(qkv_ref, gout_ref, gsum_ref, sufsum_ref, o_ref, kbd_s, vbd_s, acc_s, carry_s):
    seq = qkv_ref.shape[0]
    qi = pl.program_id(1)
    blk = SB_QBLOCK

    @pl.when(qi == 0)
    def _():
        def build(j, _):
            rows = pl.ds(pl.multiple_of(j * blk, blk), blk)
            kbd_s[j] = _blockdiag(qkv_ref[rows, WG:2 * WG], HD).astype(BF16)
            vbd_s[j] = _blockdiag(qkv_ref[rows, 2 * WG:], HD).astype(BF16)
            return 0
        lax.fori_loop(0, seq // blk, build, 0)

    q = (qkv_ref[pl.ds(pl.multiple_of(qi * blk, blk), blk), :WG] * (HD ** -0.5)).astype(BF16)
    sufsum = sufsum_ref[...]
    past = lax.broadcasted_iota(jnp.int32, (blk, blk), 1) < lax.broadcasted_iota(jnp.int32, (blk, blk), 0)

    def block(j, diagonal):
        z = _dot_nt(q, kbd_s[j])
        probs = []
        for h in range(HEADS):
            zh = z[:, h * blk:(h + 1) * blk]
            lf = -_softplus(zh)
            if diagonal:
                lf = jnp.where(past, lf, 0.0)
            cs = _dot_exact_rhs(lf, sufsum)
            carry = carry_s[:, h * blk:(h + 1) * blk]
            a = jnp.exp(lf + zh + cs[:, :blk] + carry)
            if diagonal:
                a = jnp.where(past, a, 0.0)
            carry_s[:, h * blk:(h + 1) * blk] = carry + cs[:, blk:]
            probs.append(a.astype(BF16))
        acc_s[...] += _dot(jnp.concatenate(probs, axis=1), vbd_s[j])

    acc_s[...] = jnp.zeros_like(acc_s)
    carry_s[...] = jnp.zeros_like(carry_s)
    block(qi, True)

    def cond(state):
        j, largest_carry = state
        return jnp.logical_and(j >= 0, largest_carry > SB_ZERO_LOG)

    def body(state):
        j, _ = state
        block(j, False)
        return j - 1, jnp.max(carry_s[...])

    lax.while_loop(cond, body, (qi - 1, jnp.max(carry_s[...])))

    o = acc_s[...]
    ms = _dot_exact_rhs(o * o, gsum_ref[...]) * (1.0 / HD)
    o_ref[...] = o * lax.rsqrt(ms + EPS) * gout_ref[...]


def _stick_breaking(p, gout, consts, batch, seq):
    n = batch * seq
    nq = seq // SB_QBLOCK
    full = lambda a: pl.BlockSpec(a.shape, lambda b, i: (0,) * a.ndim)
    return pl.pallas_call(
        _sb_kernel,
        grid=(batch, nq),
        in_specs=[
            pl.BlockSpec((seq, 3 * WG), lambda b, i: (b, COL_QKV_D // (3 * WG))),
            pl.BlockSpec((1, WG), lambda b, i: (0, 0)),
            full(consts["gsum"]), full(consts["sufsum"]),
        ],
        out_specs=pl.BlockSpec((SB_QBLOCK, WG), lambda b, i: (b * nq + i, 0)),
        out_shape=jax.ShapeDtypeStruct((n, WG), F32),
        scratch_shapes=[
            pltpu.VMEM((nq, HEADS * SB_QBLOCK, WG), BF16),
            pltpu.VMEM((nq, HEADS * SB_QBLOCK, WG), BF16),
            pltpu.VMEM((SB_QBLOCK, WG), F32),
            pltpu.VMEM((SB_QBLOCK, HEADS * SB_QBLOCK), F32),
        ],
        compiler_params=pltpu.CompilerParams(
            dimension_semantics=("arbitrary", "arbitrary"), vmem_limit_bytes=VMEM_LIMIT),
        name="stick_breaking",
    )(p, gout, consts["gsum"], consts["sufsum"])


def _outmlp_kernel(x_ref, ya_ref, yb_ref, yc_ref, yd_ref, mod_ref, g2_ref, gfin_ref,
                   wout_ref, w1_ref, w2_ref, o_ref, *, ff_chunk, final):
    m = mod_ref[0]
    y = jnp.concatenate([r[...].astype(BF16) for r in (ya_ref, yb_ref, yc_ref, yd_ref)], axis=1)
    x = x_ref[...] + m[2:3] * _dot(y, wout_ref[...])
    h = (_rms(x, g2_ref[...]) * (1.0 + m[4:5]) + m[3:4]).astype(BF16)
    d_ff = w1_ref.shape[1]
    ff = jnp.zeros(x.shape, F32)
    for k in range(0, d_ff, ff_chunk):
        a = jnp.maximum(_dot(h, w1_ref[:, k:k + ff_chunk]), 0.0)
        ff = ff + _dot((a * a).astype(BF16), w2_ref[k:k + ff_chunk, :])
    x = x + m[5:6] * ff
    if final:
        x = _rms(x, gfin_ref[...])
    o_ref[...] = x


def _outmlp(x2, ys, mod_l, g2, gfin, w_out, w_ff1, w_ff2, seq, tm, final):
    n, d = x2.shape
    d_ff = w_ff1.shape[1]
    tiles_per_seq = seq // tm
    tile = lambda w: pl.BlockSpec((tm, w), lambda i: (i, 0))
    const = lambda a: pl.BlockSpec(a.shape, lambda i: (0,) * a.ndim, pipeline_mode=pl.Buffered(1))
    return pl.pallas_call(
        functools.partial(_outmlp_kernel, ff_chunk=min(d_ff, 1024), final=final),
        grid=(n // tm,),
        in_specs=[
            tile(d), tile(WG), tile(WG), tile(WG), tile(WG),
            pl.BlockSpec((1, N_MOD, d), lambda i: (i // tiles_per_seq, 0, 0)),
            pl.BlockSpec((1, d), lambda i: (0, 0)),
            pl.BlockSpec((1, d), lambda i: (0, 0)),
            const(w_out), const(w_ff1), const(w_ff2),
        ],
        out_specs=tile(d),
        out_shape=jax.ShapeDtypeStruct((n, d), F32),
        compiler_params=pltpu.CompilerParams(vmem_limit_bytes=VMEM_LIMIT),
        name="outproj_mlp",
    )(x2, *ys, mod_l, g2, gfin, w_out, w_ff1, w_ff2)


def _constants():
    idx = jnp.arange(WG)
    gsum = (idx[:, None] // HD == idx[None, :] // HD).astype(BF16)
    ab = jnp.arange(AB_PAD)[:, None]
    lane = jnp.arange(2 * WG)[None, :]
    expand = (ab == (lane // WG) * HEADS + (lane % WG) // HD).astype(BF16)
    r = jnp.arange(PREP_ROWS)
    ltri = ((r[:, None] // CHUNK == r[None, :] // CHUNK) & (r[:, None] >= r[None, :])).astype(BF16)
    k = jnp.arange(SB_QBLOCK)
    suf = (k[:, None] > k[None, :]).astype(BF16)
    sufsum = jnp.concatenate([suf, jnp.ones((SB_QBLOCK, SB_QBLOCK), BF16)], axis=1)
    return {"gsum": gsum, "expand": expand, "ltri": ltri, "sufsum": sufsum}


def _pad_w_in(w_in):
    o = 0
    parts = {}
    for name, size in (("qkv_a", 3 * WG), ("ab", 2 * HEADS), ("gate_a", WG), ("u_b", WG), ("v_b", WG),
                       ("x_c", WG), ("qkv_d", 3 * WG)):
        parts[name] = w_in[..., o:o + size]
        o += size
    ab = jnp.pad(parts["ab"], ((0, 0), (0, 0), (0, AB_PAD - 2 * HEADS)))
    return jnp.concatenate([parts["qkv_a"], parts["qkv_d"], parts["gate_a"], parts["u_b"], parts["v_b"],
                            parts["x_c"], ab], axis=-1).astype(BF16)


def _per_head_lanes(v):
    return jnp.repeat(v, HD, axis=-1)[..., None, :]


def kernel(x, c, w_mod, b_mod, g_norm1, g_norm2, w_in, conv_a, a_log, dt_bias, g_out_a, g_ln_b, w_sg, b_sg,
           w_pool, s_pool, g_out_d, w_out, w_ff1, w_ff2, g_final):
    batch, seq, d = x.shape
    depth = w_mod.shape[0]
    assert seq % PREP_ROWS == 0 and d == HEADS * WG
    tm = min(512, seq)
    consts = _constants()

    mod = _modulation(c, w_mod, b_mod).reshape(depth, batch, N_MOD, d)
    w_in_p = _pad_w_in(w_in)
    w_out_b, w_ff1_b, w_ff2_b = w_out.astype(BF16), w_ff1.astype(BF16), w_ff2.astype(BF16)
    alog_b, dtb_b = _per_head_lanes(a_log), _per_head_lanes(dt_bias)
    wsg_cat = jnp.transpose(w_sg, (0, 2, 1, 3)).reshape(depth, SG_BLOCK, HEADS * SG_BLOCK)
    bsg_packed = jnp.repeat(jnp.transpose(b_sg, (0, 2, 1)), HD, axis=-1)
    wpool_bd = (jnp.eye(HEADS, dtype=F32)[None, :, None, :, None] * w_pool[:, :, :, None, :]).reshape(
        depth, WG, WG).astype(BF16)
    row = lambda v: v[:, None, :]

    x2 = x.reshape(batch * seq, d)
    for l in range(depth):
        p = _inproj(x2, mod[l], row(g_norm1)[l], w_in_p[l], seq, tm)
        y_a = _gdn(p, conv_a[l], alog_b[l], dtb_b[l], row(g_out_a)[l], consts, batch, seq)
        y_b, y_c = _sgpool(p, row(g_ln_b)[l], wsg_cat[l], bsg_packed[l], wpool_bd[l], row(s_pool)[l], consts,
                           batch, seq)
        y_d = _stick_breaking(p, row(g_out_d)[l], consts, batch, seq)
        x2 = _outmlp(x2, (y_a, y_b, y_c, y_d), mod[l], row(g_norm2)[l], g_final[None, :], w_out_b[l],
                     w_ff1_b[l], w_ff2_b[l], seq, tm, final=(l == depth - 1))
    return x2.reshape(batch, seq, d)
```

```python
import functools

import jax
import jax.numpy as jnp
from jax import lax
from jax.experimental import pallas as pl
from jax.experimental.pallas import tpu as pltpu

F32 = jnp.float32
BF16 = jnp.bfloat16
MIX_OUT = BF16

EPS = 1e-6
N_MOD = 6
HEADS = 4
HD = 64
WG = HEADS * HD
CHUNK = 64
CONV_W = 4
SG_BLOCK = 128
SG_INTERLEAVE = 4
SB_QBLOCK = 128
POOL_WINDOWS = (2, 4, 8, 16)
POOL_HALO = 16
PREP_ROWS = 256
GDN_INTERLEAVE = 8
SB_GROUP = 4
LOG2_E = 1.4426950408889634
SB_ZERO_LOG2 = -160.0

COL_QKV_A, COL_QKV_D, COL_GATE_A, COL_U_B, COL_V_B, COL_X_C, COL_AB = 0, 768, 1536, 1792, 2048, 2304, 2560
AB_PAD = 128
P_PAD = COL_AB + AB_PAD

VMEM_LIMIT = 56 * 1024 * 1024


def _dot(a, b):
    return jnp.dot(a, b, preferred_element_type=F32)


def _dot_nt(a, b):
    return lax.dot_general(a, b, (((1,), (1,)), ((), ())), preferred_element_type=F32)


def _split(x):
    hi = x.astype(BF16)
    lo = (x - hi.astype(F32)).astype(BF16)
    return hi, lo


def _dot_exact_rhs(x, m):
    hi, lo = _split(x)
    return _dot(hi, m) + _dot(lo, m)


def _dot_exact_lhs(m, x):
    hi, lo = _split(x)
    return _dot(m, hi) + _dot(m, lo)


def _sigmoid(x):
    return 1.0 / (1.0 + jnp.exp(-x))


def _softplus(x):
    return jnp.maximum(x, 0.0) + jnp.log(1.0 + jnp.exp(-jnp.abs(x)))


def _lane_head(shape):
    return lax.broadcasted_iota(jnp.int32, shape, 1) // HD


def _blockdiag(y, width):
    lane_blk = lax.broadcasted_iota(jnp.int32, y.shape, 1) // width
    return jnp.concatenate([jnp.where(lane_blk == h, y, 0.0) for h in range(HEADS)], axis=0)


def _shift_rows(x, k):
    return pltpu.roll(x, k, axis=0)


def _mod_kernel(c_ref, w_ref, b_ref, o_ref):
    c = c_ref[...]
    cond = c * _sigmoid(c)
    o_ref[0] = _dot(cond.astype(BF16), w_ref[0].astype(BF16)) + b_ref[0]


def _modulation(c, w_mod, b_mod):
    depth, d, n6 = w_mod.shape
    b = c.shape[0]
    nblk = n6 // d
    return pl.pallas_call(
        _mod_kernel,
        grid=(depth, nblk),
        in_specs=[
            pl.BlockSpec((b, d), lambda l, j: (0, 0)),
            pl.BlockSpec((1, d, d), lambda l, j: (l, 0, j)),
            pl.BlockSpec((1, 1, d), lambda l, j: (l, 0, j)),
        ],
        out_specs=pl.BlockSpec((1, b, d), lambda l, j: (l, 0, j)),
        out_shape=jax.ShapeDtypeStruct((depth, b, n6), F32),
        name="modulation",
    )(c, w_mod, b_mod.reshape(depth, 1, n6))


def _rms(x, g):
    return x * lax.rsqrt(jnp.mean(x * x, axis=-1, keepdims=True) + EPS) * g


def _inproj_kernel(x_ref, mod_ref, g_ref, w_ref, o_ref):
    m = mod_ref[0]
    h = _rms(x_ref[...], g_ref[...]) * (1.0 + m[1:2]) + m[0:1]
    o_ref[...] = _dot(h.astype(BF16), w_ref[...])


def _inproj(x2, mod_l, g1, w_in_p, seq, tm):
    n, d = x2.shape
    tiles_per_seq = seq // tm
    return pl.pallas_call(
        _inproj_kernel,
        grid=(n // tm,),
        in_specs=[
            pl.BlockSpec((tm, d), lambda i: (i, 0)),
            pl.BlockSpec((1, N_MOD, d), lambda i: (i // tiles_per_seq, 0, 0)),
            pl.BlockSpec((1, d), lambda i: (0, 0)),
            pl.BlockSpec((d, P_PAD), lambda i: (0, 0)),
        ],
        out_specs=pl.BlockSpec((tm, P_PAD), lambda i: (i, 0)),
        out_shape=jax.ShapeDtypeStruct((n, P_PAD), F32),
        compiler_params=pltpu.CompilerParams(vmem_limit_bytes=VMEM_LIMIT),
        name="inproj",
    )(x2, mod_l, g1, w_in_p)


def _gdn_kernel(qkv_ref, ab_ref, gate_ref, convw_ref, alog_ref, dtb_ref, gout_ref,
                gsum_ref, expand_ref, ltri_ref, o_ref,
                q_s, k_s, v_s, beta_s, gc_s, o_s, u_s, wq_s, qk_s, kd_s, state_s):
    seq = qkv_ref.shape[0]
    gsum = gsum_ref[...]

    def prep(r, _):
        r0 = pl.multiple_of(r * PREP_ROWS, PREP_ROWS)
        cur = qkv_ref[pl.ds(r0, PREP_ROWS), :]
        prev = qkv_ref[pl.ds(pl.multiple_of(jnp.maximum(r0 - 8, 0), 8), 8), :]
        prev = jnp.where(r > 0, prev, 0.0)
        ext = jnp.concatenate([prev, cur], axis=0)
        acc = cur * convw_ref[CONV_W - 1:CONV_W, :]
        for k in range(1, CONV_W):
            acc = acc + _shift_rows(ext, k)[8:, :] * convw_ref[CONV_W - 1 - k:CONV_W - k, :]
        y = acc * _sigmoid(acc)
        q, kk, v = y[:, :WG], y[:, WG:2 * WG], y[:, 2 * WG:]
        q = q * lax.rsqrt(_dot_exact_rhs(q * q, gsum) + EPS) * (HD ** -0.5)
        kk = kk * lax.rsqrt(_dot_exact_rhs(kk * kk, gsum) + EPS)
        abx = _dot_exact_rhs(ab_ref[pl.ds(r0, PREP_ROWS), :], expand_ref[...])
        a_b, b_b = abx[:, :WG], abx[:, WG:]
        g = -jnp.exp(alog_ref[...]) * _softplus(a_b + dtb_ref[...])
        rows = pl.ds(r0, PREP_ROWS)
        q_s[rows, :] = q
        k_s[rows, :] = kk
        v_s[rows, :] = v
        beta_s[rows, :] = _sigmoid(b_b)
        gc_s[rows, :] = _dot_exact_lhs(ltri_ref[...], g)
        return 0

    lax.fori_loop(0, seq // PREP_ROWS, prep, 0)

    state_s[...] = jnp.zeros_like(state_s)
    shape = (CHUNK, WG)
    row = lax.broadcasted_iota(jnp.int32, shape, 0)
    col = lax.broadcasted_iota(jnp.int32, shape, 1) % CHUNK
    eye = row == col
    incl = row >= col
    strict = row > col

    def bd(y):
        return _blockdiag(y, HD).astype(BF16)

    def intra_stages(i):
        chunks = [i * GDN_INTERLEAVE + k for k in range(GDN_INTERLEAVE)]
        rows = [pl.ds(pl.multiple_of(c * CHUNK, CHUNK), CHUNK) for c in chunks]
        lows, ts = [], []
        for c, r in zip(chunks, rows):
            qn, kn, gc = q_s[r, :], k_s[r, :], gc_s[r, :]
            kb = kn * beta_s[r, :]
            a = _dot_nt(jnp.concatenate([kb, qn], axis=0).astype(BF16), bd(kn))
            yield
            gc_row = jnp.sum(jnp.where(eye, gc, 0.0), axis=0, keepdims=True)
            decay = jnp.where(incl, jnp.exp(gc - gc_row), 0.0)
            low = jnp.where(strict, a[:CHUNK] * decay, 0.0)
            qk_s[r, :] = (a[CHUNK:] * decay).astype(BF16)
            wq_s[pl.ds(pl.multiple_of(c * 2 * CHUNK + CHUNK, CHUNK), CHUNK), :] = (qn * jnp.exp(gc)).astype(BF16)
            kd_t = (kn * jnp.exp(gc[CHUNK - 1:CHUNK, :] - gc)).T
            kd_s[r, :] = jnp.concatenate([kd_t[h * HD:(h + 1) * HD, :] for h in range(HEADS)],
                                         axis=1).astype(BF16)
            lows.append(low)
            ts.append(jnp.where(eye, 1.0, 0.0) - jnp.where(row // 2 == col // 2, low, 0.0))
        s = 2
        while s < CHUNK:
            join = jnp.logical_and(row // (2 * s) == col // (2 * s), row // s != col // s)
            tes = []
            for t, low in zip(ts, lows):
                tes.append(_dot(t.astype(BF16), bd(jnp.where(join, low, 0.0))))
                yield
            for k, te in enumerate(tes):
                ts[k] = ts[k] - _dot(te.astype(BF16), bd(ts[k]))
                yield
            s *= 2
        for c, r, t in zip(chunks, rows, ts):
            beta, gc = beta_s[r, :], gc_s[r, :]
            rhs = jnp.concatenate([bd(v_s[r, :] * beta), bd(k_s[r, :] * beta * jnp.exp(gc))], axis=1)
            uw = _dot(t.astype(BF16), rhs)
            yield
            u_s[r, :] = uw[:, :WG]
            wq_s[pl.ds(pl.multiple_of(c * 2 * CHUNK, 2 * CHUNK), CHUNK), :] = uw[:, WG:].astype(BF16)

    def scan_stages(first, count):
        state = state_s[...]
        for k in range(count):
            c = first + k
            rows = pl.ds(pl.multiple_of(c * CHUNK, CHUNK), CHUNK)
            wq = _dot(wq_s[pl.ds(pl.multiple_of(c * 2 * CHUNK, 2 * CHUNK), 2 * CHUNK), :], bd(state))
            yield
            v_new = u_s[rows, :] - wq[:CHUNK]
            ou = _dot(jnp.concatenate([qk_s[rows, :], kd_s[rows, :]], axis=0), bd(v_new))
            yield
            o_s[rows, :] = wq[CHUNK:] + ou[:CHUNK]
            gl = gc_s[pl.ds(pl.multiple_of(c * CHUNK + CHUNK - 8, 8), 8), :][7:8, :]
            state = state * jnp.exp(gl) + ou[CHUNK:]
        state_s[...] = state

    def drain(stages):
        for _ in stages:
            pass

    def interleave(main, side, main_per_side):
        main_live = side_live = True
        while main_live or side_live:
            if side_live:
                side_live = next(side, 0) is None
            for _ in range(main_per_side):
                if main_live:
                    main_live = next(main, 0) is None

    n_groups = seq // (CHUNK * GDN_INTERLEAVE)
    intra_matmuls = GDN_INTERLEAVE * (2 + 2 * (CHUNK.bit_length() - 2))
    scan_matmuls = GDN_INTERLEAVE * 2

    drain(intra_stages(0))

    def merged(g, _):
        interleave(intra_stages(g), scan_stages((g - 1) * GDN_INTERLEAVE, GDN_INTERLEAVE),
                   intra_matmuls // scan_matmuls)
        return 0

    lax.fori_loop(1, n_groups, merged, 0)

    def last_scan(c, _):
        drain(scan_stages(c, 1))
        return 0

    lax.fori_loop((n_groups - 1) * GDN_INTERLEAVE, n_groups * GDN_INTERLEAVE, last_scan, 0)

    def post(r, _):
        rows = pl.ds(pl.multiple_of(r * PREP_ROWS, PREP_ROWS), PREP_ROWS)
        o = o_s[rows, :]
        ms = _dot_exact_rhs(o * o, gsum) * (1.0 / HD)
        gate = gate_ref[rows, :]
        y = o * lax.rsqrt(ms + EPS) * gout_ref[...] * (gate * _sigmoid(gate))
        o_ref[rows, :] = y.astype(o_ref.dtype)
        return 0

    lax.fori_loop(0, seq // PREP_ROWS, post, 0)


def _gdn(p, conv_w, alog_b, dtb_b, gout, consts, batch, seq):
    n = batch * seq
    vec = lambda w: pl.BlockSpec((1, w), lambda b: (0, 0))
    full = lambda a: pl.BlockSpec(a.shape, lambda b: (0,) * a.ndim)
    return pl.pallas_call(
        _gdn_kernel,
        grid=(batch,),
        in_specs=[
            pl.BlockSpec((seq, 3 * WG), lambda b: (b, COL_QKV_A // (3 * WG))),
            pl.BlockSpec((seq, AB_PAD), lambda b: (b, COL_AB // AB_PAD)),
            pl.BlockSpec((seq, WG), lambda b: (b, COL_GATE_A // WG)),
            full(conv_w), vec(WG), vec(WG), vec(WG),
            full(consts["gsum"]), full(consts["expand"]), full(consts["ltri"]),
        ],
        out_specs=pl.BlockSpec((seq, WG), lambda b: (b, 0)),
        out_shape=jax.ShapeDtypeStruct((n, WG), MIX_OUT),
        scratch_shapes=[pltpu.VMEM((seq, WG), F32)] * 7 + [
            pltpu.VMEM((2 * seq, WG), BF16), pltpu.VMEM((seq, WG), BF16), pltpu.VMEM((seq, WG), BF16),
            pltpu.VMEM((CHUNK, WG), F32)],
        compiler_params=pltpu.CompilerParams(vmem_limit_bytes=VMEM_LIMIT),
        name="gated_deltanet",
    )(p, p, p, conv_w, alog_b, dtb_b, gout, consts["gsum"], consts["expand"], consts["ltri"])


def _sgpool_kernel(u_ref, v_ref, x_ref, gln_ref, wsg_ref, bsg_ref, wpool_ref, spool_ref, gsum_ref,
                   yb_ref, yc_ref):
    seq = u_ref.shape[0]
    gsum = gsum_ref[...]
    t_chunk = lax.broadcasted_iota(jnp.int32, wsg_ref.shape, 0) // CHUNK
    s_chunk = (lax.broadcasted_iota(jnp.int32, wsg_ref.shape, 1) % SG_BLOCK) // CHUNK
    wm = jnp.where(t_chunk >= s_chunk, wsg_ref[...], 0.0).astype(BF16)

    def gate_blocks(i, _):
        rows = [pl.ds(pl.multiple_of((i * SG_INTERLEAVE + k) * SG_BLOCK, SG_BLOCK), SG_BLOCK)
                for k in range(SG_INTERLEAVE)]
        vs = [v_ref[r, :] for r in rows]
        vcs = [v - _dot_exact_rhs(v, gsum) * (1.0 / HD) for v in vs]
        vrs = [_dot_exact_rhs(vc * vc, gsum) * (1.0 / HD) for vc in vcs]
        vns = [vc * lax.rsqrt(var + EPS) * gln_ref[...] for vc, var in zip(vcs, vrs)]
        ss = [_dot(wm, _blockdiag(vn, HD).astype(BF16)) + bsg_ref[...] for vn in vns]
        for r, s in zip(rows, ss):
            yb_ref[r, :] = (u_ref[r, :] * s).astype(yb_ref.dtype)
        return 0

    lax.fori_loop(0, seq // (SG_BLOCK * SG_INTERLEAVE), gate_blocks, 0)

    lane_grp = _lane_head((PREP_ROWS, WG))
    win = jnp.zeros((PREP_ROWS, WG), jnp.int32)
    for g, w in enumerate(POOL_WINDOWS):
        win = jnp.where(lane_grp == g, w, win)

    def pool_block(r, _):
        r0 = pl.multiple_of(r * PREP_ROWS, PREP_ROWS)
        cur = x_ref[pl.ds(r0, PREP_ROWS), :]
        prev = x_ref[pl.ds(pl.multiple_of(jnp.maximum(r0 - POOL_HALO, 0), POOL_HALO), POOL_HALO), :]
        prev = jnp.where(r > 0, prev, 0.0)
        s = jnp.concatenate([prev, cur], axis=0)
        total = jnp.zeros((PREP_ROWS, WG), F32)
        k = 1
        for g, w in enumerate(POOL_WINDOWS):
            while k < w:
                s = s + _shift_rows(s, k)
                k *= 2
            total = jnp.where(lane_grp == g, s[POOL_HALO:, :], total)
        t = r0 + lax.broadcasted_iota(jnp.int32, (PREP_ROWS, WG), 0)
        count = jnp.minimum(t + 1, win).astype(F32)
        pooled = total / count - cur
        yc_ref[pl.ds(r0, PREP_ROWS), :] = (_dot(pooled.astype(BF16), wpool_ref[...]) * spool_ref[...]).astype(
            yc_ref.dtype)
        return 0

    lax.fori_loop(0, seq // PREP_ROWS, pool_block, 0)


def _sgpool(p, gln, wsg_cat, bsg_packed, wpool_bd, spool, consts, batch, seq):
    n = batch * seq
    vec = lambda w: pl.BlockSpec((1, w), lambda b: (0, 0))
    full = lambda a: pl.BlockSpec(a.shape, lambda b: (0,) * a.ndim)
    col = lambda c: pl.BlockSpec((seq, WG), lambda b: (b, c // WG))
    out = pl.BlockSpec((seq, WG), lambda b: (b, 0))
    return pl.pallas_call(
        _sgpool_kernel,
        grid=(batch,),
        in_specs=[col(COL_U_B), col(COL_V_B), col(COL_X_C), vec(WG), full(wsg_cat), full(bsg_packed),
                  full(wpool_bd), vec(WG), full(consts["gsum"])],
        out_specs=[out, out],
        out_shape=[jax.ShapeDtypeStruct((n, WG), MIX_OUT)] * 2,
        compiler_params=pltpu.CompilerParams(vmem_limit_bytes=VMEM_LIMIT),
        name="spatial_gating_pool",
    )(p, p, p, gln, wsg_cat, bsg_packed, wpool_bd, spool, consts["gsum"])


def _sb_kernel(qkv_ref, gout_ref, gsum_ref, sufsum_ref, o_ref, kbd_s, vbd_s, acc_s, carry_s):
    seq = qkv_ref.shape[0]
    gi = pl.program_id(1)
    blk = SB_QBLOCK

    @pl.when(gi == 0)
    def _():
        def build(j, _):
            rows = pl.ds(pl.multiple_of(j * blk, blk), blk)
            kbd_s[j] = _blockdiag(qkv_ref[rows, WG:2 * WG], HD).astype(BF16)
            vbd_s[j] = _blockdiag(qkv_ref[rows, 2 * WG:], HD).astype(BF16)
            return 0
        lax.fori_loop(0, seq // blk, build, 0)

    sufsum = sufsum_ref[...]
    past = lax.broadcasted_iota(jnp.int32, (blk, blk), 1) < lax.broadcasted_iota(jnp.int32, (blk, blk), 0)
    qblocks = [gi * SB_GROUP + k for k in range(SB_GROUP)]
    qs = [(qkv_ref[pl.ds(pl.multiple_of(qb * blk, blk), blk), :WG] * (LOG2_E * HD ** -0.5)).astype(BF16)
          for qb in qblocks]

    def step(offset, diagonal):
        js = [jnp.maximum(qb - offset, 0) for qb in qblocks]
        zs = [_dot_nt(q, kbd_s[j]) for q, j in zip(qs, js)]
        probs = []
        for k, z in enumerate(zs):
            rows = slice(k * blk, (k + 1) * blk)
            per_head = []
            for h in range(HEADS):
                lanes = slice(h * blk, (h + 1) * blk)
                zh = z[:, lanes]
                nlf = jnp.maximum(zh, 0.0) + jnp.log2(1.0 + jnp.exp2(-jnp.abs(zh)))
                ls = zh - nlf
                if diagonal:
                    nlf = jnp.where(past, nlf, 0.0)
                cs = _dot(nlf.astype(BF16), sufsum)
                carry = carry_s[rows, lanes]
                a = jnp.exp2(ls + cs[:, :blk] + carry)
                if diagonal:
                    a = jnp.where(past, a, 0.0)
                carry_s[rows, lanes] = carry + cs[:, blk:]
                per_head.append(a.astype(BF16))
            probs.append(jnp.concatenate(per_head, axis=1))
        for k, (p, j) in enumerate(zip(probs, js)):
            rows = slice(k * blk, (k + 1) * blk)
            av = _dot(p, vbd_s[j])
            if not diagonal:
                av = jnp.where(qblocks[k] >= offset, av, 0.0)
            acc_s[rows, :] += av

    def largest_live_carry(offset):
        live = jnp.full((blk, HEADS * blk), -jnp.inf, F32)
        for k, qb in enumerate(qblocks):
            live = jnp.maximum(live, jnp.where(qb >= offset, carry_s[k * blk:(k + 1) * blk, :], -jnp.inf))
        return jnp.max(live)

    acc_s[...] = jnp.zeros_like(acc_s)
    carry_s[...] = jnp.zeros_like(carry_s)
    step(0, True)

    def cond(state):
        offset, largest = state
        return jnp.logical_and(offset <= qblocks[-1], largest > SB_ZERO_LOG2)

    def body(state):
        offset, _ = state
        step(offset, False)
        return offset + 1, largest_live_carry(offset + 1)

    lax.while_loop(cond, body, (jnp.int32(1), largest_live_carry(1)))

    o = acc_s[...]
    ms = _dot_exact_rhs(o * o, gsum_ref[...]) * (1.0 / HD)
    o_ref[...] = (o * lax.rsqrt(ms + EPS) * gout_ref[...]).astype(o_ref.dtype)


def _stick_breaking(p, gout, consts, batch, seq):
    n = batch * seq
    nq = seq // SB_QBLOCK
    ng = nq // SB_GROUP
    rows = SB_GROUP * SB_QBLOCK
    full = lambda a: pl.BlockSpec(a.shape, lambda b, i: (0,) * a.ndim)
    return pl.pallas_call(
        _sb_kernel,
        grid=(batch, ng),
        in_specs=[
            pl.BlockSpec((seq, 3 * WG), lambda b, i: (b, COL_QKV_D // (3 * WG))),
            pl.BlockSpec((1, WG), lambda b, i: (0, 0)),
            full(consts["gsum"]), full(consts["sufsum"]),
        ],
        out_specs=pl.BlockSpec((rows, WG), lambda b, i: (b * ng + i, 0)),
        out_shape=jax.ShapeDtypeStruct((n, WG), MIX_OUT),
        scratch_shapes=[
            pltpu.VMEM((nq, HEADS * SB_QBLOCK, WG), BF16),
            pltpu.VMEM((nq, HEADS * SB_QBLOCK, WG), BF16),
            pltpu.VMEM((rows, WG), F32),
            pltpu.VMEM((rows, HEADS * SB_QBLOCK), F32),
        ],
        compiler_params=pltpu.CompilerParams(
            dimension_semantics=("arbitrary", "arbitrary"), vmem_limit_bytes=VMEM_LIMIT),
        name="stick_breaking",
    )(p, gout, consts["gsum"], consts["sufsum"])


def _outmlp_kernel(x_ref, ya_ref, yb_ref, yc_ref, yd_ref, mod_ref, g2_ref, gfin_ref,
                   wout_ref, w1_ref, w2_ref, o_ref, *, ff_chunk, final):
    m = mod_ref[0]
    y = jnp.concatenate([r[...].astype(BF16) for r in (ya_ref, yb_ref, yc_ref, yd_ref)], axis=1)
    x = x_ref[...] + m[2:3] * _dot(y, wout_ref[...])
    h = (_rms(x, g2_ref[...]) * (1.0 + m[4:5]) + m[3:4]).astype(BF16)
    d_ff = w1_ref.shape[1]
    ff = jnp.zeros(x.shape, F32)
    for k in range(0, d_ff, ff_chunk):
        a = jnp.maximum(_dot(h, w1_ref[:, k:k + ff_chunk]), 0.0)
        ff = ff + _dot((a * a).astype(BF16), w2_ref[k:k + ff_chunk, :])
    x = x + m[5:6] * ff
    if final:
        x = _rms(x, gfin_ref[...])
    o_ref[...] = x


def _outmlp(x2, ys, mod_l, g2, gfin, w_out, w_ff1, w_ff2, seq, tm, final):
    n, d = x2.shape
    d_ff = w_ff1.shape[1]
    tiles_per_seq = seq // tm
    tile = lambda w: pl.BlockSpec((tm, w), lambda i: (i, 0))
    const = lambda a: pl.BlockSpec(a.shape, lambda i: (0,) * a.ndim, pipeline_mode=pl.Buffered(1))
    return pl.pallas_call(
        functools.partial(_outmlp_kernel, ff_chunk=min(d_ff, 1024), final=final),
        grid=(n // tm,),
        in_specs=[
            tile(d), tile(WG), tile(WG), tile(WG), tile(WG),
            pl.BlockSpec((1, N_MOD, d), lambda i: (i // tiles_per_seq, 0, 0)),
            pl.BlockSpec((1, d), lambda i: (0, 0)),
            pl.BlockSpec((1, d), lambda i: (0, 0)),
            const(w_out), const(w_ff1), const(w_ff2),
        ],
        out_specs=tile(d),
        out_shape=jax.ShapeDtypeStruct((n, d), F32),
        compiler_params=pltpu.CompilerParams(vmem_limit_bytes=VMEM_LIMIT),
        name="outproj_mlp",
    )(x2, *ys, mod_l, g2, gfin, w_out, w_ff1, w_ff2)


def _constants():
    idx = jnp.arange(WG)
    gsum = (idx[:, None] // HD == idx[None, :] // HD).astype(BF16)
    ab = jnp.arange(AB_PAD)[:, None]
    lane = jnp.arange(2 * WG)[None, :]
    expand = (ab == (lane // WG) * HEADS + (lane % WG) // HD).astype(BF16)
    r = jnp.arange(PREP_ROWS)
    ltri = ((r[:, None] // CHUNK == r[None, :] // CHUNK) & (r[:, None] >= r[None, :])).astype(BF16)
    k = jnp.arange(SB_QBLOCK)
    suf = (k[:, None] > k[None, :]).astype(BF16)
    sufsum = -jnp.concatenate([suf, jnp.ones((SB_QBLOCK, SB_QBLOCK), BF16)], axis=1)
    return {"gsum": gsum, "expand": expand, "ltri": ltri, "sufsum": sufsum}


def _pad_w_in(w_in):
    o = 0
    parts = {}
    for name, size in (("qkv_a", 3 * WG), ("ab", 2 * HEADS), ("gate_a", WG), ("u_b", WG), ("v_b", WG),
                       ("x_c", WG), ("qkv_d", 3 * WG)):
        parts[name] = w_in[..., o:o + size]
        o += size
    ab = jnp.pad(parts["ab"], ((0, 0), (0, 0), (0, AB_PAD - 2 * HEADS)))
    return jnp.concatenate([parts["qkv_a"], parts["qkv_d"], parts["gate_a"], parts["u_b"], parts["v_b"],
                            parts["x_c"], ab], axis=-1).astype(BF16)


def _per_head_lanes(v):
    return jnp.repeat(v, HD, axis=-1)[..., None, :]


def kernel(x, c, w_mod, b_mod, g_norm1, g_norm2, w_in, conv_a, a_log, dt_bias, g_out_a, g_ln_b, w_sg, b_sg,
           w_pool, s_pool, g_out_d, w_out, w_ff1, w_ff2, g_final):
    batch, seq, d = x.shape
    depth = w_mod.shape[0]
    assert seq % PREP_ROWS == 0 and d == HEADS * WG
    tm = min(512, seq)
    consts = _constants()

    mod = _modulation(c, w_mod, b_mod).reshape(depth, batch, N_MOD, d)
    w_in_p = _pad_w_in(w_in)
    w_out_b, w_ff1_b, w_ff2_b = w_out.astype(BF16), w_ff1.astype(BF16), w_ff2.astype(BF16)
    alog_b, dtb_b = _per_head_lanes(a_log), _per_head_lanes(dt_bias)
    wsg_cat = jnp.transpose(w_sg, (0, 2, 1, 3)).reshape(depth, SG_BLOCK, HEADS * SG_BLOCK)
    bsg_packed = jnp.repeat(jnp.transpose(b_sg, (0, 2, 1)), HD, axis=-1)
    wpool_bd = (jnp.eye(HEADS, dtype=F32)[None, :, None, :, None] * w_pool[:, :, :, None, :]).reshape(
        depth, WG, WG).astype(BF16)
    row = lambda v: v[:, None, :]

    x2 = x.reshape(batch * seq, d)
    for l in range(depth):
        p = _inproj(x2, mod[l], row(g_norm1)[l], w_in_p[l], seq, tm)
        y_a = _gdn(p, conv_a[l], alog_b[l], dtb_b[l], row(g_out_a)[l], consts, batch, seq)
        y_b, y_c = _sgpool(p, row(g_ln_b)[l], wsg_cat[l], bsg_packed[l], wpool_bd[l], row(s_pool)[l], consts,
                           batch, seq)
        y_d = _stick_breaking(p, row(g_out_d)[l], consts, batch, seq)
        x2 = _outmlp(x2, (y_a, y_b, y_c, y_d), mod[l], row(g_norm2)[l], g_final[None, :], w_out_b[l],
                     w_ff1_b[l], w_ff2_b[l], seq, tm, final=(l == depth - 1))
    return x2.reshape(batch, seq, d)
```

```python
import functools

import jax
import jax.numpy as jnp
from jax import lax
from jax.experimental import pallas as pl
from jax.experimental.pallas import tpu as pltpu

F32 = jnp.float32
BF16 = jnp.bfloat16
MIX_OUT = BF16

EPS = 1e-6
N_MOD = 6
HEADS = 4
HD = 64
WG = HEADS * HD
CHUNK = 64
CONV_W = 4
SG_BLOCK = 128
SG_INTERLEAVE = 4
SB_QBLOCK = 128
POOL_WINDOWS = (2, 4, 8, 16)
POOL_HALO = 16
TOKEN_SUBTILES = 2
PREP_ROWS = 256
GDN_INTERLEAVE = 8
SB_GROUP = 8
LOG2_E = 1.4426950408889634
SB_ZERO_LOG2 = -160.0

COL_QKV_A, COL_QKV_D, COL_GATE_A, COL_U_B, COL_V_B, COL_X_C, COL_AB = 0, 768, 1536, 1792, 2048, 2304, 2560
AB_PAD = 128
P_PAD = COL_AB + AB_PAD

VMEM_LIMIT = 56 * 1024 * 1024


def _dot(a, b):
    return jnp.dot(a, b, preferred_element_type=F32)


def _dot_nt(a, b):
    return lax.dot_general(a, b, (((1,), (1,)), ((), ())), preferred_element_type=F32)


def _split(x):
    hi = x.astype(BF16)
    lo = (x - hi.astype(F32)).astype(BF16)
    return hi, lo


def _dot_exact_rhs(x, m):
    hi, lo = _split(x)
    return _dot(hi, m) + _dot(lo, m)


def _dot_exact_lhs(m, x):
    hi, lo = _split(x)
    return _dot(m, hi) + _dot(m, lo)


def _sigmoid(x):
    return 1.0 / (1.0 + jnp.exp(-x))


def _softplus(x):
    return jnp.maximum(x, 0.0) + jnp.log(1.0 + jnp.exp(-jnp.abs(x)))


def _lane_head(shape):
    return lax.broadcasted_iota(jnp.int32, shape, 1) // HD


def _blockdiag(y, width):
    lane_blk = lax.broadcasted_iota(jnp.int32, y.shape, 1) // width
    return jnp.concatenate([jnp.where(lane_blk == h, y, 0.0) for h in range(HEADS)], axis=0)


def _shift_rows(x, k):
    return pltpu.roll(x, k, axis=0)


def _mod_kernel(c_ref, w_ref, b_ref, o_ref):
    c = c_ref[...]
    cond = c * _sigmoid(c)
    o_ref[0] = _dot(cond.astype(BF16), w_ref[0].astype(BF16)) + b_ref[0]


def _modulation(c, w_mod, b_mod):
    depth, d, n6 = w_mod.shape
    b = c.shape[0]
    nblk = n6 // d
    return pl.pallas_call(
        _mod_kernel,
        grid=(depth, nblk),
        in_specs=[
            pl.BlockSpec((b, d), lambda l, j: (0, 0)),
            pl.BlockSpec((1, d, d), lambda l, j: (l, 0, j)),
            pl.BlockSpec((1, 1, d), lambda l, j: (l, 0, j)),
        ],
        out_specs=pl.BlockSpec((1, b, d), lambda l, j: (l, 0, j)),
        out_shape=jax.ShapeDtypeStruct((depth, b, n6), F32),
        name="modulation",
    )(c, w_mod, b_mod.reshape(depth, 1, n6))


def _rms(x, g):
    return x * lax.rsqrt(jnp.mean(x * x, axis=-1, keepdims=True) + EPS) * g


def _inproj_kernel(x_ref, mod_ref, g_ref, w_ref, o_ref):
    m = mod_ref[0]
    sub = x_ref.shape[0] // TOKEN_SUBTILES
    parts = [slice(k * sub, (k + 1) * sub) for k in range(TOKEN_SUBTILES)]
    hs = [(_rms(x_ref[r, :], g_ref[...]) * (1.0 + m[1:2]) + m[0:1]).astype(BF16) for r in parts]
    for r, h in zip(parts, hs):
        o_ref[r, :] = _dot(h, w_ref[...])


def _inproj(x2, mod_l, g1, w_in_p, layer, seq, tm):
    n, d = x2.shape
    tiles_per_seq = seq // tm
    return pl.pallas_call(
        _inproj_kernel,
        grid=(n // tm,),
        in_specs=[
            pl.BlockSpec((tm, d), lambda i: (i, 0)),
            pl.BlockSpec((1, N_MOD, d), lambda i: (i // tiles_per_seq, 0, 0)),
            pl.BlockSpec((1, d), lambda i: (0, 0)),
            pl.BlockSpec((None, d, P_PAD), lambda i: (layer, 0, 0)),
        ],
        out_specs=pl.BlockSpec((tm, P_PAD), lambda i: (i, 0)),
        out_shape=jax.ShapeDtypeStruct((n, P_PAD), F32),
        compiler_params=pltpu.CompilerParams(vmem_limit_bytes=VMEM_LIMIT),
        name="inproj",
    )(x2, mod_l, g1, w_in_p)


def _gdn_kernel(qkv_ref, ab_ref, gate_ref, convw_ref, alog_ref, dtb_ref, gout_ref,
                gsum_ref, expand_ref, ltri_ref, o_ref,
                q_s, k_s, v_s, beta_s, gc_s, o_s, u_s, wq_s, qk_s, kd_s, state_s):
    seq = qkv_ref.shape[0]
    gsum = gsum_ref[...]

    def prep(r, _):
        r0 = pl.multiple_of(r * PREP_ROWS, PREP_ROWS)
        cur = qkv_ref[pl.ds(r0, PREP_ROWS), :]
        prev = qkv_ref[pl.ds(pl.multiple_of(jnp.maximum(r0 - 8, 0), 8), 8), :]
        prev = jnp.where(r > 0, prev, 0.0)
        ext = jnp.concatenate([prev, cur], axis=0)
        acc = cur * convw_ref[CONV_W - 1:CONV_W, :]
        for k in range(1, CONV_W):
            acc = acc + _shift_rows(ext, k)[8:, :] * convw_ref[CONV_W - 1 - k:CONV_W - k, :]
        y = acc * _sigmoid(acc)
        q, kk, v = y[:, :WG], y[:, WG:2 * WG], y[:, 2 * WG:]
        q = q * lax.rsqrt(_dot((q * q).astype(BF16), gsum) + EPS) * (HD ** -0.5)
        kk = kk * lax.rsqrt(_dot((kk * kk).astype(BF16), gsum) + EPS)
        abx = _dot_exact_rhs(ab_ref[pl.ds(r0, PREP_ROWS), :], expand_ref[...])
        a_b, b_b = abx[:, :WG], abx[:, WG:]
        g = -jnp.exp(alog_ref[...]) * _softplus(a_b + dtb_ref[...])
        rows = pl.ds(r0, PREP_ROWS)
        q_s[rows, :] = q
        k_s[rows, :] = kk
        v_s[rows, :] = v
        beta_s[rows, :] = _sigmoid(b_b)
        gc_s[rows, :] = _dot_exact_lhs(ltri_ref[...], g)
        return 0

    lax.fori_loop(0, seq // PREP_ROWS, prep, 0)

    state_s[...] = jnp.zeros_like(state_s)
    shape = (CHUNK, WG)
    row = lax.broadcasted_iota(jnp.int32, shape, 0)
    col = lax.broadcasted_iota(jnp.int32, shape, 1) % CHUNK
    eye = row == col
    incl = row >= col
    strict = row > col

    def bd(y):
        return _blockdiag(y, HD).astype(BF16)

    def intra_stages(i):
        chunks = [i * GDN_INTERLEAVE + k for k in range(GDN_INTERLEAVE)]
        rows = [pl.ds(pl.multiple_of(c * CHUNK, CHUNK), CHUNK) for c in chunks]
        lows, ts = [], []
        for c, r in zip(chunks, rows):
            qn, kn, gc = q_s[r, :], k_s[r, :], gc_s[r, :]
            kb = kn * beta_s[r, :]
            a = _dot_nt(jnp.concatenate([kb, qn], axis=0).astype(BF16), bd(kn))
            yield
            gc_row = jnp.sum(jnp.where(eye, gc, 0.0), axis=0, keepdims=True)
            decay = jnp.where(incl, jnp.exp(gc - gc_row), 0.0)
            low = jnp.where(strict, a[:CHUNK] * decay, 0.0)
            qk_s[r, :] = (a[CHUNK:] * decay).astype(BF16)
            wq_s[pl.ds(pl.multiple_of(c * 2 * CHUNK + CHUNK, CHUNK), CHUNK), :] = (qn * jnp.exp(gc)).astype(BF16)
            kd_t = (kn * jnp.exp(gc[CHUNK - 1:CHUNK, :] - gc)).T
            kd_s[r, :] = jnp.concatenate([kd_t[h * HD:(h + 1) * HD, :] for h in range(HEADS)],
                                         axis=1).astype(BF16)
            lows.append(low)
            ts.append(jnp.where(eye, 1.0, 0.0) - jnp.where(row // 2 == col // 2, low, 0.0))
        s = 2
        while s < CHUNK:
            join = jnp.logical_and(row // (2 * s) == col // (2 * s), row // s != col // s)
            tes = []
            for t, low in zip(ts, lows):
                tes.append(_dot(t.astype(BF16), bd(jnp.where(join, low, 0.0))))
                yield
            for k, te in enumerate(tes):
                ts[k] = ts[k] - _dot(te.astype(BF16), bd(ts[k]))
                yield
            s *= 2
        for c, r, t in zip(chunks, rows, ts):
            beta, gc = beta_s[r, :], gc_s[r, :]
            rhs = jnp.concatenate([bd(v_s[r, :] * beta), bd(k_s[r, :] * beta * jnp.exp(gc))], axis=1)
            uw = _dot(t.astype(BF16), rhs)
            yield
            u_s[r, :] = uw[:, :WG]
            wq_s[pl.ds(pl.multiple_of(c * 2 * CHUNK, 2 * CHUNK), CHUNK), :] = uw[:, WG:].astype(BF16)

    def scan_stages(first, count):
        state = state_s[...]
        for k in range(count):
            c = first + k
            rows = pl.ds(pl.multiple_of(c * CHUNK, CHUNK), CHUNK)
            wq = _dot(wq_s[pl.ds(pl.multiple_of(c * 2 * CHUNK, 2 * CHUNK), 2 * CHUNK), :], bd(state))
            yield
            v_new = u_s[rows, :] - wq[:CHUNK]
            ou = _dot(jnp.concatenate([qk_s[rows, :], kd_s[rows, :]], axis=0), bd(v_new))
            yield
            o_s[rows, :] = wq[CHUNK:] + ou[:CHUNK]
            gl = gc_s[pl.ds(pl.multiple_of(c * CHUNK + CHUNK - 8, 8), 8), :][7:8, :]
            state = state * jnp.exp(gl) + ou[CHUNK:]
        state_s[...] = state

    def drain(stages):
        for _ in stages:
            pass

    def interleave(main, side, main_per_side):
        main_live = side_live = True
        while main_live or side_live:
            if side_live:
                side_live = next(side, 0) is None
            for _ in range(main_per_side):
                if main_live:
                    main_live = next(main, 0) is None

    n_groups = seq // (CHUNK * GDN_INTERLEAVE)
    intra_matmuls = GDN_INTERLEAVE * (2 + 2 * (CHUNK.bit_length() - 2))
    scan_matmuls = GDN_INTERLEAVE * 2

    drain(intra_stages(0))

    def merged(g, _):
        interleave(intra_stages(g), scan_stages((g - 1) * GDN_INTERLEAVE, GDN_INTERLEAVE),
                   intra_matmuls // scan_matmuls)
        return 0

    lax.fori_loop(1, n_groups, merged, 0)

    def post(r):
        rows = pl.ds(pl.multiple_of(r * PREP_ROWS, PREP_ROWS), PREP_ROWS)
        o = o_s[rows, :]
        ms = _dot_exact_rhs(o * o, gsum) * (1.0 / HD)
        gate = gate_ref[rows, :]
        y = o * lax.rsqrt(ms + EPS) * gout_ref[...] * (gate * _sigmoid(gate))
        o_ref[rows, :] = y.astype(o_ref.dtype)

    n_tiles = seq // PREP_ROWS
    done_tiles = (n_groups - 1) * GDN_INTERLEAVE * CHUNK // PREP_ROWS
    overlapped = min(done_tiles, GDN_INTERLEAVE)
    last_first = (n_groups - 1) * GDN_INTERLEAVE

    def last_scan_with_post(i, _):
        post(i)
        drain(scan_stages(last_first + i, 1))
        return 0

    def last_scan(c, _):
        drain(scan_stages(c, 1))
        return 0

    def post_only(r, _):
        post(r)
        return 0

    lax.fori_loop(0, overlapped, last_scan_with_post, 0)
    lax.fori_loop(last_first + overlapped, last_first + GDN_INTERLEAVE, last_scan, 0)
    lax.fori_loop(overlapped, n_tiles, post_only, 0)


def _gdn(p, conv_w, alog_b, dtb_b, gout, consts, batch, seq):
    n = batch * seq
    vec = lambda w: pl.BlockSpec((1, w), lambda b: (0, 0))
    full = lambda a: pl.BlockSpec(a.shape, lambda b: (0,) * a.ndim)
    return pl.pallas_call(
        _gdn_kernel,
        grid=(batch,),
        in_specs=[
            pl.BlockSpec((seq, 3 * WG), lambda b: (b, COL_QKV_A // (3 * WG))),
            pl.BlockSpec((seq, AB_PAD), lambda b: (b, COL_AB // AB_PAD)),
            pl.BlockSpec((seq, WG), lambda b: (b, COL_GATE_A // WG)),
            full(conv_w), vec(WG), vec(WG), vec(WG),
            full(consts["gsum"]), full(consts["expand"]), full(consts["ltri"]),
        ],
        out_specs=pl.BlockSpec((seq, WG), lambda b: (b, 0)),
        out_shape=jax.ShapeDtypeStruct((n, WG), MIX_OUT),
        scratch_shapes=[pltpu.VMEM((seq, WG), F32)] * 7 + [
            pltpu.VMEM((2 * seq, WG), BF16), pltpu.VMEM((seq, WG), BF16), pltpu.VMEM((seq, WG), BF16),
            pltpu.VMEM((CHUNK, WG), F32)],
        compiler_params=pltpu.CompilerParams(vmem_limit_bytes=VMEM_LIMIT),
        name="gated_deltanet",
    )(p, p, p, conv_w, alog_b, dtb_b, gout, consts["gsum"], consts["expand"], consts["ltri"])


def _sgpool_kernel(u_ref, v_ref, x_ref, gln_ref, wsg_ref, bsg_ref, wpool_ref, spool_ref, gsum_ref,
                   yb_ref, yc_ref):
    seq = u_ref.shape[0]
    gsum = gsum_ref[...]
    t_chunk = lax.broadcasted_iota(jnp.int32, wsg_ref.shape, 0) // CHUNK
    s_chunk = (lax.broadcasted_iota(jnp.int32, wsg_ref.shape, 1) % SG_BLOCK) // CHUNK
    wm = jnp.where(t_chunk >= s_chunk, wsg_ref[...], 0.0).astype(BF16)

    def gate_blocks(i, _):
        rows = [pl.ds(pl.multiple_of((i * SG_INTERLEAVE + k) * SG_BLOCK, SG_BLOCK), SG_BLOCK)
                for k in range(SG_INTERLEAVE)]
        vs = [v_ref[r, :] for r in rows]
        vcs = [v - _dot_exact_rhs(v, gsum) * (1.0 / HD) for v in vs]
        vrs = [_dot_exact_rhs(vc * vc, gsum) * (1.0 / HD) for vc in vcs]
        vns = [vc * lax.rsqrt(var + EPS) * gln_ref[...] for vc, var in zip(vcs, vrs)]
        ss = [_dot(wm, _blockdiag(vn, HD).astype(BF16)) + bsg_ref[...] for vn in vns]
        for r, s in zip(rows, ss):
            yb_ref[r, :] = (u_ref[r, :] * s).astype(yb_ref.dtype)
        return 0

    lax.fori_loop(0, seq // (SG_BLOCK * SG_INTERLEAVE), gate_blocks, 0)

    lane_grp = _lane_head((PREP_ROWS, WG))
    win = jnp.zeros((PREP_ROWS, WG), jnp.int32)
    for g, w in enumerate(POOL_WINDOWS):
        win = jnp.where(lane_grp == g, w, win)

    def pool_block(r, _):
        r0 = pl.multiple_of(r * PREP_ROWS, PREP_ROWS)
        cur = x_ref[pl.ds(r0, PREP_ROWS), :]
        prev = x_ref[pl.ds(pl.multiple_of(jnp.maximum(r0 - POOL_HALO, 0), POOL_HALO), POOL_HALO), :]
        prev = jnp.where(r > 0, prev, 0.0)
        s = jnp.concatenate([prev, cur], axis=0)
        total = jnp.zeros((PREP_ROWS, WG), F32)
        k = 1
        for g, w in enumerate(POOL_WINDOWS):
            while k < w:
                s = s + _shift_rows(s, k)
                k *= 2
            total = jnp.where(lane_grp == g, s[POOL_HALO:, :], total)
        t = r0 + lax.broadcasted_iota(jnp.int32, (PREP_ROWS, WG), 0)
        count = jnp.minimum(t + 1, win).astype(F32)
        pooled = total / count - cur
        yc_ref[pl.ds(r0, PREP_ROWS), :] = (_dot(pooled.astype(BF16), wpool_ref[...]) * spool_ref[...]).astype(
            yc_ref.dtype)
        return 0

    lax.fori_loop(0, seq // PREP_ROWS, pool_block, 0)


def _sgpool(p, gln, wsg_cat, bsg_packed, wpool_bd, spool, consts, batch, seq):
    n = batch * seq
    vec = lambda w: pl.BlockSpec((1, w), lambda b: (0, 0))
    full = lambda a: pl.BlockSpec(a.shape, lambda b: (0,) * a.ndim)
    col = lambda c: pl.BlockSpec((seq, WG), lambda b: (b, c // WG))
    out = pl.BlockSpec((seq, WG), lambda b: (b, 0))
    return pl.pallas_call(
        _sgpool_kernel,
        grid=(batch,),
        in_specs=[col(COL_U_B), col(COL_V_B), col(COL_X_C), vec(WG), full(wsg_cat), full(bsg_packed),
                  full(wpool_bd), vec(WG), full(consts["gsum"])],
        out_specs=[out, out],
        out_shape=[jax.ShapeDtypeStruct((n, WG), MIX_OUT)] * 2,
        compiler_params=pltpu.CompilerParams(vmem_limit_bytes=VMEM_LIMIT),
        name="spatial_gating_pool",
    )(p, p, p, gln, wsg_cat, bsg_packed, wpool_bd, spool, consts["gsum"])


def _sb_kernel(qkv_ref, gout_ref, gsum_ref, sufsum_ref, o_ref, kbd_s, vbd_s, acc_s, carry_s):
    seq = qkv_ref.shape[0]
    gi = pl.program_id(1)
    blk = SB_QBLOCK

    @pl.when(gi == 0)
    def _():
        def build(j, _):
            rows = pl.ds(pl.multiple_of(j * blk, blk), blk)
            kbd_s[j] = _blockdiag(qkv_ref[rows, WG:2 * WG], HD).astype(BF16)
            vbd_s[j] = _blockdiag(qkv_ref[rows, 2 * WG:], HD).astype(BF16)
            return 0
        lax.fori_loop(0, seq // blk, build, 0)

    sufsum = sufsum_ref[...]
    past = lax.broadcasted_iota(jnp.int32, (blk, blk), 1) < lax.broadcasted_iota(jnp.int32, (blk, blk), 0)
    qblocks = [gi * SB_GROUP + k for k in range(SB_GROUP)]
    qs = [(qkv_ref[pl.ds(pl.multiple_of(qb * blk, blk), blk), :WG] * (LOG2_E * HD ** -0.5)).astype(BF16)
          for qb in qblocks]

    def step(offset, diagonal):
        js = [jnp.maximum(qb - offset, 0) for qb in qblocks]
        zs = [_dot_nt(q, kbd_s[j]) for q, j in zip(qs, js)]
        probs = []
        for k, z in enumerate(zs):
            rows = slice(k * blk, (k + 1) * blk)
            per_head = []
            for h in range(HEADS):
                lanes = slice(h * blk, (h + 1) * blk)
                zh = z[:, lanes]
                nlf = jnp.maximum(zh, 0.0) + jnp.log2(1.0 + jnp.exp2(-jnp.abs(zh)))
                ls = zh - nlf
                if diagonal:
                    nlf = jnp.where(past, nlf, 0.0)
                cs = _dot(nlf.astype(BF16), sufsum)
                carry = carry_s[rows, lanes]
                a = jnp.exp2(ls + cs[:, :blk] + carry)
                if diagonal:
                    a = jnp.where(past, a, 0.0)
                carry_s[rows, lanes] = carry + cs[:, blk:]
                per_head.append(a.astype(BF16))
            probs.append(jnp.concatenate(per_head, axis=1))
        for k, (p, j) in enumerate(zip(probs, js)):
            rows = slice(k * blk, (k + 1) * blk)
            av = _dot(p, vbd_s[j])
            if not diagonal:
                av = jnp.where(qblocks[k] >= offset, av, 0.0)
            acc_s[rows, :] += av

    def largest_live_carry(offset):
        live = jnp.full((blk, HEADS * blk), -jnp.inf, F32)
        for k, qb in enumerate(qblocks):
            live = jnp.maximum(live, jnp.where(qb >= offset, carry_s[k * blk:(k + 1) * blk, :], -jnp.inf))
        return jnp.max(live)

    acc_s[...] = jnp.zeros_like(acc_s)
    carry_s[...] = jnp.zeros_like(carry_s)
    step(0, True)

    def cond(state):
        offset, largest = state
        return jnp.logical_and(offset <= qblocks[-1], largest > SB_ZERO_LOG2)

    def body(state):
        offset, _ = state
        step(offset, False)
        return offset + 1, largest_live_carry(offset + 1)

    lax.while_loop(cond, body, (jnp.int32(1), largest_live_carry(1)))

    o = acc_s[...]
    ms = _dot_exact_rhs(o * o, gsum_ref[...]) * (1.0 / HD)
    o_ref[...] = (o * lax.rsqrt(ms + EPS) * gout_ref[...]).astype(o_ref.dtype)


def _stick_breaking(p, gout, consts, batch, seq):
    n = batch * seq
    nq = seq // SB_QBLOCK
    ng = nq // SB_GROUP
    rows = SB_GROUP * SB_QBLOCK
    full = lambda a: pl.BlockSpec(a.shape, lambda b, i: (0,) * a.ndim)
    return pl.pallas_call(
        _sb_kernel,
        grid=(batch, ng),
        in_specs=[
            pl.BlockSpec((seq, 3 * WG), lambda b, i: (b, COL_QKV_D // (3 * WG))),
            pl.BlockSpec((1, WG), lambda b, i: (0, 0)),
            full(consts["gsum"]), full(consts["sufsum"]),
        ],
        out_specs=pl.BlockSpec((rows, WG), lambda b, i: (b * ng + i, 0)),
        out_shape=jax.ShapeDtypeStruct((n, WG), MIX_OUT),
        scratch_shapes=[
            pltpu.VMEM((nq, HEADS * SB_QBLOCK, WG), BF16),
            pltpu.VMEM((nq, HEADS * SB_QBLOCK, WG), BF16),
            pltpu.VMEM((rows, WG), F32),
            pltpu.VMEM((rows, HEADS * SB_QBLOCK), F32),
        ],
        compiler_params=pltpu.CompilerParams(
            dimension_semantics=("arbitrary", "arbitrary"), vmem_limit_bytes=VMEM_LIMIT),
        name="stick_breaking",
    )(p, gout, consts["gsum"], consts["sufsum"])


def _outmlp_kernel(x_ref, ya_ref, yb_ref, yc_ref, yd_ref, mod_ref, g2_ref, gfin_ref,
                   wout_ref, w1_ref, w2_ref, o_ref, *, ff_chunk, final):
    m = mod_ref[0]
    sub = x_ref.shape[0] // TOKEN_SUBTILES
    parts = [slice(k * sub, (k + 1) * sub) for k in range(TOKEN_SUBTILES)]
    ys = [jnp.concatenate([ref[r, :].astype(BF16) for ref in (ya_ref, yb_ref, yc_ref, yd_ref)], axis=1)
          for r in parts]
    xs = [x_ref[r, :] + m[2:3] * _dot(y, wout_ref[...]) for r, y in zip(parts, ys)]
    hs = [(_rms(x, g2_ref[...]) * (1.0 + m[4:5]) + m[3:4]).astype(BF16) for x in xs]
    d_ff = w1_ref.shape[1]
    ffs = [jnp.zeros(x.shape, F32) for x in xs]
    for k in range(0, d_ff, ff_chunk):
        acts = [jnp.maximum(_dot(h, w1_ref[:, k:k + ff_chunk]), 0.0) for h in hs]
        ffs = [ff + _dot((a * a).astype(BF16), w2_ref[k:k + ff_chunk, :]) for ff, a in zip(ffs, acts)]
    for r, x, ff in zip(parts, xs, ffs):
        x = x + m[5:6] * ff
        if final:
            x = _rms(x, gfin_ref[...])
        o_ref[r, :] = x


def _outmlp(x2, ys, mod_l, g2, gfin, w_out, w_ff1, w_ff2, layer, seq, tm, final):
    n, d = x2.shape
    d_ff = w_ff1.shape[-1]
    tiles_per_seq = seq // tm
    tile = lambda w: pl.BlockSpec((tm, w), lambda i: (i, 0))
    const = lambda a: pl.BlockSpec((None,) + a.shape[1:], lambda i: (layer, 0, 0), pipeline_mode=pl.Buffered(1))
    return pl.pallas_call(
        functools.partial(_outmlp_kernel, ff_chunk=min(d_ff, 1024), final=final),
        grid=(n // tm,),
        in_specs=[
            tile(d), tile(WG), tile(WG), tile(WG), tile(WG),
            pl.BlockSpec((1, N_MOD, d), lambda i: (i // tiles_per_seq, 0, 0)),
            pl.BlockSpec((1, d), lambda i: (0, 0)),
            pl.BlockSpec((1, d), lambda i: (0, 0)),
            const(w_out), const(w_ff1), const(w_ff2),
        ],
        out_specs=tile(d),
        out_shape=jax.ShapeDtypeStruct((n, d), F32),
        compiler_params=pltpu.CompilerParams(vmem_limit_bytes=VMEM_LIMIT),
        name="outproj_mlp",
    )(x2, *ys, mod_l, g2, gfin, w_out, w_ff1, w_ff2)


def _constants():
    idx = jnp.arange(WG)
    gsum = (idx[:, None] // HD == idx[None, :] // HD).astype(BF16)
    ab = jnp.arange(AB_PAD)[:, None]
    lane = jnp.arange(2 * WG)[None, :]
    expand = (ab == (lane // WG) * HEADS + (lane % WG) // HD).astype(BF16)
    r = jnp.arange(PREP_ROWS)
    ltri = ((r[:, None] // CHUNK == r[None, :] // CHUNK) & (r[:, None] >= r[None, :])).astype(BF16)
    k = jnp.arange(SB_QBLOCK)
    suf = (k[:, None] > k[None, :]).astype(BF16)
    sufsum = -jnp.concatenate([suf, jnp.ones((SB_QBLOCK, SB_QBLOCK), BF16)], axis=1)
    return {"gsum": gsum, "expand": expand, "ltri": ltri, "sufsum": sufsum}


def _pad_w_in(w_in):
    o = 0
    parts = {}
    for name, size in (("qkv_a", 3 * WG), ("ab", 2 * HEADS), ("gate_a", WG), ("u_b", WG), ("v_b", WG),
                       ("x_c", WG), ("qkv_d", 3 * WG)):
        parts[name] = w_in[..., o:o + size]
        o += size
    ab = jnp.pad(parts["ab"], ((0, 0), (0, 0), (0, AB_PAD - 2 * HEADS)))
    return jnp.concatenate([parts["qkv_a"], parts["qkv_d"], parts["gate_a"], parts["u_b"], parts["v_b"],
                            parts["x_c"], ab], axis=-1).astype(BF16)


def _per_head_lanes(v):
    return jnp.repeat(v, HD, axis=-1)[..., None, :]


def kernel(x, c, w_mod, b_mod, g_norm1, g_norm2, w_in, conv_a, a_log, dt_bias, g_out_a, g_ln_b, w_sg, b_sg,
           w_pool, s_pool, g_out_d, w_out, w_ff1, w_ff2, g_final):
    batch, seq, d = x.shape
    depth = w_mod.shape[0]
    assert seq % (SB_GROUP * SB_QBLOCK) == 0 and seq % (GDN_INTERLEAVE * CHUNK) == 0 and d == HEADS * WG
    tm = min(512, seq)
    consts = _constants()

    mod = _modulation(c, w_mod, b_mod).reshape(depth, batch, N_MOD, d)
    w_in_p = _pad_w_in(w_in)
    w_out_b, w_ff1_b, w_ff2_b = w_out.astype(BF16), w_ff1.astype(BF16), w_ff2.astype(BF16)
    alog_b, dtb_b = _per_head_lanes(a_log), _per_head_lanes(dt_bias)
    wsg_cat = jnp.transpose(w_sg, (0, 2, 1, 3)).reshape(depth, SG_BLOCK, HEADS * SG_BLOCK)
    bsg_packed = jnp.repeat(jnp.transpose(b_sg, (0, 2, 1)), HD, axis=-1)
    wpool_bd = (jnp.eye(HEADS, dtype=F32)[None, :, None, :, None] * w_pool[:, :, :, None, :]).reshape(
        depth, WG, WG).astype(BF16)
    row = lambda v: v[:, None, :]

    x2 = x.reshape(batch * seq, d)
    for l in range(depth):
        p = _inproj(x2, mod[l], row(g_norm1)[l], w_in_p, l, seq, tm)
        y_a = _gdn(p, conv_a[l], alog_b[l], dtb_b[l], row(g_out_a)[l], consts, batch, seq)
        y_b, y_c = _sgpool(p, row(g_ln_b)[l], wsg_cat[l], bsg_packed[l], wpool_bd[l], row(s_pool)[l], consts,
                           batch, seq)
        y_d = _stick_breaking(p, row(g_out_d)[l], consts, batch, seq)
        x2 = _outmlp(x2, (y_a, y_b, y_c, y_d), mod[l], row(g_norm2)[l], g_final[None, :], w_out_b,
                     w_ff1_b, w_ff2_b, l, seq, tm, final=(l == depth - 1))
    return x2.reshape(batch, seq, d)
```

```python
import functools

import jax
import jax.numpy as jnp
from jax import lax
from jax.experimental import pallas as pl
from jax.experimental.pallas import tpu as pltpu

F32 = jnp.float32
BF16 = jnp.bfloat16
MIX_OUT = BF16

EPS = 1e-6
N_MOD = 6
HEADS = 4
HD = 64
WG = HEADS * HD
CHUNK = 64
CONV_W = 4
SG_BLOCK = 128
SB_QBLOCK = 128
POOL_WINDOWS = (2, 4, 8, 16)
POOL_HALO = 16
TOKEN_SUBTILES = 2
PREP_ROWS = 256
GDN_INTERLEAVE = 8
SB_GROUP = 8
LOG2_E = 1.4426950408889634
SB_ZERO_LOG2 = -160.0

COL_QKV_A, COL_GATE_A, COL_AB, COL_QKV_D, COL_U_B, COL_V_B, COL_X_C = 0, 768, 1024, 1152, 1920, 2176, 2432
AB_PAD = 128
P_GDN = COL_QKV_D
P_SB = COL_U_B
P_PAD = COL_X_C + 256

VMEM_LIMIT = 56 * 1024 * 1024


def _dot(a, b):
    return jnp.dot(a, b, preferred_element_type=F32)


def _dot_nt(a, b):
    return lax.dot_general(a, b, (((1,), (1,)), ((), ())), preferred_element_type=F32)


def _split(x):
    hi = x.astype(BF16)
    lo = (x - hi.astype(F32)).astype(BF16)
    return hi, lo


def _dot_exact_rhs(x, m):
    hi, lo = _split(x)
    return _dot(hi, m) + _dot(lo, m)


def _dot_exact_lhs(m, x):
    hi, lo = _split(x)
    return _dot(m, hi) + _dot(m, lo)


def _sigmoid(x):
    return 1.0 / (1.0 + jnp.exp(-x))


def _softplus(x):
    return jnp.maximum(x, 0.0) + jnp.log(1.0 + jnp.exp(-jnp.abs(x)))


def _lane_head(shape):
    return lax.broadcasted_iota(jnp.int32, shape, 1) // HD


def _blockdiag(y, width):
    lane_blk = lax.broadcasted_iota(jnp.int32, y.shape, 1) // width
    return jnp.concatenate([jnp.where(lane_blk == h, y, 0.0) for h in range(HEADS)], axis=0)


def _shift_rows(x, k):
    return pltpu.roll(x, k, axis=0)


def _mod_kernel(c_ref, w_ref, b_ref, o_ref):
    c = c_ref[...]
    cond = c * _sigmoid(c)
    o_ref[0] = _dot(cond.astype(BF16), w_ref[0].astype(BF16)) + b_ref[0]


def _modulation(c, w_mod, b_mod):
    depth, d, n6 = w_mod.shape
    b = c.shape[0]
    nblk = n6 // d
    return pl.pallas_call(
        _mod_kernel,
        grid=(depth, nblk),
        in_specs=[
            pl.BlockSpec((b, d), lambda l, j: (0, 0)),
            pl.BlockSpec((1, d, d), lambda l, j: (l, 0, j)),
            pl.BlockSpec((1, 1, d), lambda l, j: (l, 0, j)),
        ],
        out_specs=pl.BlockSpec((1, b, d), lambda l, j: (l, 0, j)),
        out_shape=jax.ShapeDtypeStruct((depth, b, n6), F32),
        name="modulation",
    )(c, w_mod, b_mod.reshape(depth, 1, n6))


def _rms(x, g):
    return x * lax.rsqrt(jnp.mean(x * x, axis=-1, keepdims=True) + EPS) * g


def _sg_pool_stages(rows, u, v, xc, halo, t0, gln_ref, wsg_ref, bsg_ref, wpool_ref, spool_ref, gsum, yb_ref,
                    yc_ref):
    blocks = [slice(k * SG_BLOCK, (k + 1) * SG_BLOCK) for k in range(rows // SG_BLOCK)]
    t_chunk = lax.broadcasted_iota(jnp.int32, wsg_ref.shape, 0) // CHUNK
    s_chunk = (lax.broadcasted_iota(jnp.int32, wsg_ref.shape, 1) % SG_BLOCK) // CHUNK
    wm = jnp.where(t_chunk >= s_chunk, wsg_ref[...], 0.0).astype(BF16)
    whole = slice(0, rows)
    mean = _dot(v(whole).astype(BF16), gsum) * (1.0 / HD)
    yield
    vc = v(whole) - mean
    var = _dot((vc * vc).astype(BF16), gsum) * (1.0 / HD)
    lane_grp = _lane_head((PREP_ROWS, WG))
    win = jnp.zeros((PREP_ROWS, WG), jnp.int32)
    for g, w in enumerate(POOL_WINDOWS):
        win = jnp.where(lane_grp == g, w, win)
    pooled = []
    for piece in range(rows // PREP_ROWS):
        cur = xc(slice(piece * PREP_ROWS, (piece + 1) * PREP_ROWS))
        prev = halo if piece == 0 else xc(slice(piece * PREP_ROWS - POOL_HALO, piece * PREP_ROWS))
        s = jnp.concatenate([prev, cur], axis=0)
        total = jnp.zeros((PREP_ROWS, WG), F32)
        k = 1
        for g, w in enumerate(POOL_WINDOWS):
            while k < w:
                s = s + _shift_rows(s, k)
                k *= 2
            total = jnp.where(lane_grp == g, s[POOL_HALO:, :], total)
        t = t0 + piece * PREP_ROWS + lax.broadcasted_iota(jnp.int32, (PREP_ROWS, WG), 0)
        count = jnp.minimum(t + 1, win).astype(F32)
        pooled.append((total / count - cur).astype(BF16))
    yield
    vn = vc * lax.rsqrt(var + EPS) * gln_ref[...]
    vns = [_blockdiag(vn[r, :], HD).astype(BF16) for r in blocks]
    for r, vn in zip(blocks, vns):
        yb_ref[r, :] = (u(r) * (_dot(wm, vn) + bsg_ref[...])).astype(yb_ref.dtype)
    for piece, pl_ in enumerate(pooled):
        yc_ref[piece * PREP_ROWS:(piece + 1) * PREP_ROWS, :] = (
            _dot(pl_, wpool_ref[...]) * spool_ref[...]).astype(yc_ref.dtype)


def _inproj_kernel(x_ref, mod_ref, g_ref, w_ref, gln_ref, wsg_ref, bsg_ref, wpool_ref, spool_ref, gsum_ref,
                   o_ref, osb_ref, yb_ref, yc_ref, uvx_s, halo_s, *, tiles_per_seq):
    i = pl.program_id(0)
    tm = x_ref.shape[0]

    @pl.when(i == 0)
    def _():
        uvx_s[...] = jnp.zeros_like(uvx_s)
        halo_s[...] = jnp.zeros_like(halo_s)

    prev_tile = jnp.maximum(i - 1, 0)
    prev_buf = (i + 1) % 2
    u, v, xc = (functools.partial(lambda k, r: uvx_s[prev_buf, r, k * WG:(k + 1) * WG], k) for k in range(3))
    halo = jnp.where(prev_tile % tiles_per_seq == 0, 0.0, halo_s[...])
    t0 = (prev_tile % tiles_per_seq) * tm
    side = _sg_pool_stages(tm, u, v, xc, halo, t0, gln_ref, wsg_ref, bsg_ref, wpool_ref, spool_ref, gsum_ref[...],
                           yb_ref, yc_ref)

    m = mod_ref[0]
    sub = tm // TOKEN_SUBTILES
    parts = [slice(k * sub, (k + 1) * sub) for k in range(TOKEN_SUBTILES)]
    hs = [(_rms(x_ref[r, :], g_ref[...]) * (1.0 + m[1:2]) + m[0:1]).astype(BF16) for r in parts]
    next(side)
    new_halo = xc(slice(tm - POOL_HALO, tm))
    for r, h in zip(parts, hs):
        p = _dot(h, w_ref[...])
        o_ref[r, :] = p[:, :P_GDN]
        osb_ref[r, :] = p[:, P_GDN:P_SB].astype(osb_ref.dtype)
        uvx_s[i % 2, r, :] = p[:, P_SB:]
        next(side, None)
    for _ in side:
        pass
    halo_s[...] = new_halo


def _inproj(x2, mod_l, g1, w_in_p, layer, sg, consts, seq, tm):
    n, d = x2.shape
    tiles_per_seq = seq // tm
    n_tiles = n // tm
    cur = lambda i: jnp.minimum(i, n_tiles - 1)
    full = lambda a: pl.BlockSpec(a.shape, lambda i: (0,) * a.ndim)
    mix_out = pl.BlockSpec((tm, WG), lambda i: (jnp.maximum(i - 1, 0), 0))
    return pl.pallas_call(
        functools.partial(_inproj_kernel, tiles_per_seq=tiles_per_seq),
        grid=(n_tiles + 1,),
        in_specs=[
            pl.BlockSpec((tm, d), lambda i: (cur(i), 0)),
            pl.BlockSpec((1, N_MOD, d), lambda i: (cur(i) // tiles_per_seq, 0, 0)),
            pl.BlockSpec((1, d), lambda i: (0, 0)),
            pl.BlockSpec((None, d, P_PAD), lambda i: (layer, 0, 0)),
        ] + [full(a) for a in sg] + [full(consts["gsum"])],
        out_specs=[pl.BlockSpec((tm, P_GDN), lambda i: (cur(i), 0)),
                   pl.BlockSpec((tm, P_SB - P_GDN), lambda i: (cur(i), 0)), mix_out, mix_out],
        out_shape=[jax.ShapeDtypeStruct((n, P_GDN), F32), jax.ShapeDtypeStruct((n, P_SB - P_GDN), BF16)]
        + [jax.ShapeDtypeStruct((n, WG), MIX_OUT)] * 2,
        scratch_shapes=[pltpu.VMEM((2, tm, 3 * WG), F32), pltpu.VMEM((POOL_HALO, WG), F32)],
        compiler_params=pltpu.CompilerParams(dimension_semantics=("arbitrary",), vmem_limit_bytes=VMEM_LIMIT),
        name="inproj_gating_pool",
    )(x2, mod_l, g1, w_in_p, *sg, consts["gsum"])


def _gdn_kernel(qkv_ref, ab_ref, gate_ref, convw_ref, alog_ref, dtb_ref, gout_ref,
                gsum_ref, expand_ref, ltri_ref, o_ref,
                q_s, k_s, v_s, beta_s, gc_s, o_s, u_s, wq_s, qk_s, kd_s, state_s):
    seq = qkv_ref.shape[0]
    gsum = gsum_ref[...]

    def prep(r, _):
        r0 = pl.multiple_of(r * PREP_ROWS, PREP_ROWS)
        cur = qkv_ref[pl.ds(r0, PREP_ROWS), :]
        prev = qkv_ref[pl.ds(pl.multiple_of(jnp.maximum(r0 - 8, 0), 8), 8), :]
        prev = jnp.where(r > 0, prev, 0.0)
        ext = jnp.concatenate([prev, cur], axis=0)
        acc = cur * convw_ref[CONV_W - 1:CONV_W, :]
        for k in range(1, CONV_W):
            acc = acc + _shift_rows(ext, k)[8:, :] * convw_ref[CONV_W - 1 - k:CONV_W - k, :]
        y = acc * _sigmoid(acc)
        q, kk, v = y[:, :WG], y[:, WG:2 * WG], y[:, 2 * WG:]
        q = q * lax.rsqrt(_dot((q * q).astype(BF16), gsum) + EPS) * (HD ** -0.5)
        kk = kk * lax.rsqrt(_dot((kk * kk).astype(BF16), gsum) + EPS)
        abx = _dot_exact_rhs(ab_ref[pl.ds(r0, PREP_ROWS), :], expand_ref[...])
        a_b, b_b = abx[:, :WG], abx[:, WG:]
        g = -jnp.exp(alog_ref[...]) * _softplus(a_b + dtb_ref[...])
        rows = pl.ds(r0, PREP_ROWS)
        q_s[rows, :] = q
        k_s[rows, :] = kk
        v_s[rows, :] = v
        beta_s[rows, :] = _sigmoid(b_b)
        gc_s[rows, :] = _dot_exact_lhs(ltri_ref[...], g)
        return 0

    lax.fori_loop(0, seq // PREP_ROWS, prep, 0)

    state_s[...] = jnp.zeros_like(state_s)
    shape = (CHUNK, WG)
    row = lax.broadcasted_iota(jnp.int32, shape, 0)
    col = lax.broadcasted_iota(jnp.int32, shape, 1) % CHUNK
    eye = row == col
    incl = row >= col
    strict = row > col

    def bd(y):
        return _blockdiag(y, HD).astype(BF16)

    def intra_stages(i):
        chunks = [i * GDN_INTERLEAVE + k for k in range(GDN_INTERLEAVE)]
        rows = [pl.ds(pl.multiple_of(c * CHUNK, CHUNK), CHUNK) for c in chunks]
        lows, ts = [], []
        for c, r in zip(chunks, rows):
            qn, kn, gc = q_s[r, :], k_s[r, :], gc_s[r, :]
            kb = kn * beta_s[r, :]
            a = _dot_nt(jnp.concatenate([kb, qn], axis=0).astype(BF16), bd(kn))
            yield
            gc_row = jnp.sum(jnp.where(eye, gc, 0.0), axis=0, keepdims=True)
            decay = jnp.where(incl, jnp.exp(gc - gc_row), 0.0)
            low = jnp.where(strict, a[:CHUNK] * decay, 0.0)
            qk_s[r, :] = (a[CHUNK:] * decay).astype(BF16)
            wq_s[pl.ds(pl.multiple_of(c * 2 * CHUNK + CHUNK, CHUNK), CHUNK), :] = (qn * jnp.exp(gc)).astype(BF16)
            kd_t = (kn * jnp.exp(gc[CHUNK - 1:CHUNK, :] - gc)).T
            kd_s[r, :] = jnp.concatenate([kd_t[h * HD:(h + 1) * HD, :] for h in range(HEADS)],
                                         axis=1).astype(BF16)
            lows.append(low)
            ts.append(jnp.where(eye, 1.0, 0.0) - jnp.where(row // 2 == col // 2, low, 0.0))
        s = 2
        while s < CHUNK:
            join = jnp.logical_and(row // (2 * s) == col // (2 * s), row // s != col // s)
            tes = []
            for t, low in zip(ts, lows):
                tes.append(_dot(t.astype(BF16), bd(jnp.where(join, low, 0.0))))
                yield
            for k, te in enumerate(tes):
                ts[k] = ts[k] - _dot(te.astype(BF16), bd(ts[k]))
                yield
            s *= 2
        for c, r, t in zip(chunks, rows, ts):
            beta, gc = beta_s[r, :], gc_s[r, :]
            rhs = jnp.concatenate([bd(v_s[r, :] * beta), bd(k_s[r, :] * beta * jnp.exp(gc))], axis=1)
            uw = _dot(t.astype(BF16), rhs)
            yield
            u_s[r, :] = uw[:, :WG]
            wq_s[pl.ds(pl.multiple_of(c * 2 * CHUNK, 2 * CHUNK), CHUNK), :] = uw[:, WG:].astype(BF16)

    def scan_stages(first, count):
        state = state_s[...]
        for k in range(count):
            c = first + k
            rows = pl.ds(pl.multiple_of(c * CHUNK, CHUNK), CHUNK)
            wq = _dot(wq_s[pl.ds(pl.multiple_of(c * 2 * CHUNK, 2 * CHUNK), 2 * CHUNK), :], bd(state))
            yield
            v_new = u_s[rows, :] - wq[:CHUNK]
            ou = _dot(jnp.concatenate([qk_s[rows, :], kd_s[rows, :]], axis=0), bd(v_new))
            yield
            o_s[rows, :] = wq[CHUNK:] + ou[:CHUNK]
            gl = gc_s[pl.ds(pl.multiple_of(c * CHUNK + CHUNK - 8, 8), 8), :][7:8, :]
            state = state * jnp.exp(gl) + ou[CHUNK:]
        state_s[...] = state

    def drain(stages):
        for _ in stages:
            pass

    def interleave(main, side, main_per_side):
        main_live = side_live = True
        while main_live or side_live:
            if side_live:
                side_live = next(side, 0) is None
            for _ in range(main_per_side):
                if main_live:
                    main_live = next(main, 0) is None

    n_groups = seq // (CHUNK * GDN_INTERLEAVE)
    intra_matmuls = GDN_INTERLEAVE * (2 + 2 * (CHUNK.bit_length() - 2))
    scan_matmuls = GDN_INTERLEAVE * 2

    drain(intra_stages(0))

    def merged(g, _):
        interleave(intra_stages(g), scan_stages((g - 1) * GDN_INTERLEAVE, GDN_INTERLEAVE),
                   intra_matmuls // scan_matmuls)
        return 0

    lax.fori_loop(1, n_groups, merged, 0)

    def post(r):
        rows = pl.ds(pl.multiple_of(r * PREP_ROWS, PREP_ROWS), PREP_ROWS)
        o = o_s[rows, :]
        ms = _dot_exact_rhs(o * o, gsum) * (1.0 / HD)
        gate = gate_ref[rows, :]
        y = o * lax.rsqrt(ms + EPS) * gout_ref[...] * (gate * _sigmoid(gate))
        o_ref[rows, :] = y.astype(o_ref.dtype)

    n_tiles = seq // PREP_ROWS
    done_tiles = (n_groups - 1) * GDN_INTERLEAVE * CHUNK // PREP_ROWS
    overlapped = min(done_tiles, GDN_INTERLEAVE)
    last_first = (n_groups - 1) * GDN_INTERLEAVE

    def last_scan_with_post(i, _):
        post(i)
        drain(scan_stages(last_first + i, 1))
        return 0

    def last_scan(c, _):
        drain(scan_stages(c, 1))
        return 0

    def post_only(r, _):
        post(r)
        return 0

    lax.fori_loop(0, overlapped, last_scan_with_post, 0)
    lax.fori_loop(last_first + overlapped, last_first + GDN_INTERLEAVE, last_scan, 0)
    lax.fori_loop(overlapped, n_tiles, post_only, 0)


def _gdn(p, conv_w, alog_b, dtb_b, gout, consts, batch, seq):
    n = batch * seq
    vec = lambda w: pl.BlockSpec((1, w), lambda b: (0, 0))
    full = lambda a: pl.BlockSpec(a.shape, lambda b: (0,) * a.ndim)
    return pl.pallas_call(
        _gdn_kernel,
        grid=(batch,),
        in_specs=[
            pl.BlockSpec((seq, 3 * WG), lambda b: (b, COL_QKV_A // (3 * WG))),
            pl.BlockSpec((seq, AB_PAD), lambda b: (b, COL_AB // AB_PAD)),
            pl.BlockSpec((seq, WG), lambda b: (b, COL_GATE_A // WG)),
            full(conv_w), vec(WG), vec(WG), vec(WG),
            full(consts["gsum"]), full(consts["expand"]), full(consts["ltri"]),
        ],
        out_specs=pl.BlockSpec((seq, WG), lambda b: (b, 0)),
        out_shape=jax.ShapeDtypeStruct((n, WG), MIX_OUT),
        scratch_shapes=[pltpu.VMEM((seq, WG), F32)] * 7 + [
            pltpu.VMEM((2 * seq, WG), BF16), pltpu.VMEM((seq, WG), BF16), pltpu.VMEM((seq, WG), BF16),
            pltpu.VMEM((CHUNK, WG), F32)],
        compiler_params=pltpu.CompilerParams(vmem_limit_bytes=VMEM_LIMIT),
        name="gated_deltanet",
    )(p, p, p, conv_w, alog_b, dtb_b, gout, consts["gsum"], consts["expand"], consts["ltri"])


def _sb_kernel(qkv_ref, gout_ref, gsum_ref, sufsum_ref, o_ref, kbd_s, vbd_s, acc_s, carry_s):
    seq = qkv_ref.shape[0]
    gi = pl.program_id(1)
    blk = SB_QBLOCK

    @pl.when(gi == 0)
    def _():
        def build(j, _):
            rows = pl.ds(pl.multiple_of(j * blk, blk), blk)
            kbd_s[j] = _blockdiag(qkv_ref[rows, WG:2 * WG].astype(F32), HD).astype(BF16)
            vbd_s[j] = _blockdiag(qkv_ref[rows, 2 * WG:].astype(F32), HD).astype(BF16)
            return 0
        lax.fori_loop(0, seq // blk, build, 0)

    sufsum = sufsum_ref[...]
    past = lax.broadcasted_iota(jnp.int32, (blk, blk), 1) < lax.broadcasted_iota(jnp.int32, (blk, blk), 0)
    qblocks = [gi * SB_GROUP + k for k in range(SB_GROUP)]
    qs = [(qkv_ref[pl.ds(pl.multiple_of(qb * blk, blk), blk), :WG].astype(F32) * (LOG2_E * HD ** -0.5)).astype(BF16)
          for qb in qblocks]

    def step(offset, diagonal):
        js = [jnp.maximum(qb - offset, 0) for qb in qblocks]
        zs = [_dot_nt(q, kbd_s[j]) for q, j in zip(qs, js)]
        probs = []
        for k, z in enumerate(zs):
            rows = slice(k * blk, (k + 1) * blk)
            per_head = []
            for h in range(HEADS):
                lanes = slice(h * blk, (h + 1) * blk)
                zh = z[:, lanes]
                nlf = jnp.maximum(zh, 0.0) + jnp.log2(1.0 + jnp.exp2(-jnp.abs(zh)))
                ls = zh - nlf
                if diagonal:
                    nlf = jnp.where(past, nlf, 0.0)
                cs = _dot(nlf.astype(BF16), sufsum)
                carry = carry_s[rows, lanes]
                a = jnp.exp2(ls + cs[:, :blk] + carry)
                if diagonal:
                    a = jnp.where(past, a, 0.0)
                carry_s[rows, lanes] = carry + cs[:, blk:]
                per_head.append(a.astype(BF16))
            probs.append(jnp.concatenate(per_head, axis=1))
        for k, (p, j) in enumerate(zip(probs, js)):
            rows = slice(k * blk, (k + 1) * blk)
            av = _dot(p, vbd_s[j])
            if not diagonal:
                av = jnp.where(qblocks[k] >= offset, av, 0.0)
            acc_s[rows, :] += av

    def largest_live_carry(offset):
        live = jnp.full((blk, HEADS * blk), -jnp.inf, F32)
        for k, qb in enumerate(qblocks):
            live = jnp.maximum(live, jnp.where(qb >= offset, carry_s[k * blk:(k + 1) * blk, :], -jnp.inf))
        return jnp.max(live)

    acc_s[...] = jnp.zeros_like(acc_s)
    carry_s[...] = jnp.zeros_like(carry_s)
    step(0, True)

    def cond(state):
        offset, largest = state
        return jnp.logical_and(offset <= qblocks[-1], largest > SB_ZERO_LOG2)

    def body(state):
        offset, _ = state
        step(offset, False)
        return offset + 1, largest_live_carry(offset + 1)

    lax.while_loop(cond, body, (jnp.int32(1), largest_live_carry(1)))

    o = acc_s[...]
    ms = _dot_exact_rhs(o * o, gsum_ref[...]) * (1.0 / HD)
    o_ref[...] = (o * lax.rsqrt(ms + EPS) * gout_ref[...]).astype(o_ref.dtype)


def _stick_breaking(p, gout, consts, batch, seq):
    n = batch * seq
    nq = seq // SB_QBLOCK
    ng = nq // SB_GROUP
    rows = SB_GROUP * SB_QBLOCK
    full = lambda a: pl.BlockSpec(a.shape, lambda b, i: (0,) * a.ndim)
    return pl.pallas_call(
        _sb_kernel,
        grid=(batch, ng),
        in_specs=[
            pl.BlockSpec((seq, 3 * WG), lambda b, i: (b, 0)),
            pl.BlockSpec((1, WG), lambda b, i: (0, 0)),
            full(consts["gsum"]), full(consts["sufsum"]),
        ],
        out_specs=pl.BlockSpec((rows, WG), lambda b, i: (b * ng + i, 0)),
        out_shape=jax.ShapeDtypeStruct((n, WG), MIX_OUT),
        scratch_shapes=[
            pltpu.VMEM((nq, HEADS * SB_QBLOCK, WG), BF16),
            pltpu.VMEM((nq, HEADS * SB_QBLOCK, WG), BF16),
            pltpu.VMEM((rows, WG), F32),
            pltpu.VMEM((rows, HEADS * SB_QBLOCK), F32),
        ],
        compiler_params=pltpu.CompilerParams(
            dimension_semantics=("arbitrary", "arbitrary"), vmem_limit_bytes=VMEM_LIMIT),
        name="stick_breaking",
    )(p, gout, consts["gsum"], consts["sufsum"])


def _outmlp_kernel(x_ref, ya_ref, yb_ref, yc_ref, yd_ref, mod_ref, g2_ref, gfin_ref,
                   wout_ref, w1_ref, w2_ref, o_ref, *, ff_chunk, final):
    m = mod_ref[0]
    sub = x_ref.shape[0] // TOKEN_SUBTILES
    parts = [slice(k * sub, (k + 1) * sub) for k in range(TOKEN_SUBTILES)]
    ys = [jnp.concatenate([ref[r, :].astype(BF16) for ref in (ya_ref, yb_ref, yc_ref, yd_ref)], axis=1)
          for r in parts]
    xs = [x_ref[r, :] + m[2:3] * _dot(y, wout_ref[...]) for r, y in zip(parts, ys)]
    hs = [(_rms(x, g2_ref[...]) * (1.0 + m[4:5]) + m[3:4]).astype(BF16) for x in xs]
    d_ff = w1_ref.shape[1]
    ffs = [jnp.zeros(x.shape, F32) for x in xs]
    for k in range(0, d_ff, ff_chunk):
        acts = [jnp.maximum(_dot(h, w1_ref[:, k:k + ff_chunk]), 0.0) for h in hs]
        ffs = [ff + _dot((a * a).astype(BF16), w2_ref[k:k + ff_chunk, :]) for ff, a in zip(ffs, acts)]
    for r, x, ff in zip(parts, xs, ffs):
        x = x + m[5:6] * ff
        if final:
            x = _rms(x, gfin_ref[...])
        o_ref[r, :] = x


def _outmlp(x2, ys, mod_l, g2, gfin, w_out, w_ff1, w_ff2, layer, seq, tm, final):
    n, d = x2.shape
    d_ff = w_ff1.shape[-1]
    tiles_per_seq = seq // tm
    tile = lambda w: pl.BlockSpec((tm, w), lambda i: (i, 0))
    const = lambda a: pl.BlockSpec((None,) + a.shape[1:], lambda i: (layer, 0, 0), pipeline_mode=pl.Buffered(1))
    return pl.pallas_call(
        functools.partial(_outmlp_kernel, ff_chunk=min(d_ff, 1024), final=final),
        grid=(n // tm,),
        in_specs=[
            tile(d), tile(WG), tile(WG), tile(WG), tile(WG),
            pl.BlockSpec((1, N_MOD, d), lambda i: (i // tiles_per_seq, 0, 0)),
            pl.BlockSpec((1, d), lambda i: (0, 0)),
            pl.BlockSpec((1, d), lambda i: (0, 0)),
            const(w_out), const(w_ff1), const(w_ff2),
        ],
        out_specs=tile(d),
        out_shape=jax.ShapeDtypeStruct((n, d), F32),
        compiler_params=pltpu.CompilerParams(vmem_limit_bytes=VMEM_LIMIT),
        name="outproj_mlp",
    )(x2, *ys, mod_l, g2, gfin, w_out, w_ff1, w_ff2)


def _constants():
    idx = jnp.arange(WG)
    gsum = (idx[:, None] // HD == idx[None, :] // HD).astype(BF16)
    ab = jnp.arange(AB_PAD)[:, None]
    lane = jnp.arange(2 * WG)[None, :]
    expand = (ab == (lane // WG) * HEADS + (lane % WG) // HD).astype(BF16)
    r = jnp.arange(PREP_ROWS)
    ltri = ((r[:, None] // CHUNK == r[None, :] // CHUNK) & (r[:, None] >= r[None, :])).astype(BF16)
    k = jnp.arange(SB_QBLOCK)
    suf = (k[:, None] > k[None, :]).astype(BF16)
    sufsum = -jnp.concatenate([suf, jnp.ones((SB_QBLOCK, SB_QBLOCK), BF16)], axis=1)
    return {"gsum": gsum, "expand": expand, "ltri": ltri, "sufsum": sufsum}


def _pad_w_in(w_in):
    o = 0
    parts = {}
    for name, size in (("qkv_a", 3 * WG), ("ab", 2 * HEADS), ("gate_a", WG), ("u_b", WG), ("v_b", WG),
                       ("x_c", WG), ("qkv_d", 3 * WG)):
        parts[name] = w_in[..., o:o + size]
        o += size
    ab = jnp.pad(parts["ab"], ((0, 0), (0, 0), (0, AB_PAD - 2 * HEADS)))
    return jnp.concatenate([parts["qkv_a"], parts["gate_a"], ab, parts["qkv_d"], parts["u_b"], parts["v_b"],
                            parts["x_c"]], axis=-1).astype(BF16)


def _per_head_lanes(v):
    return jnp.repeat(v, HD, axis=-1)[..., None, :]


def kernel(x, c, w_mod, b_mod, g_norm1, g_norm2, w_in, conv_a, a_log, dt_bias, g_out_a, g_ln_b, w_sg, b_sg,
           w_pool, s_pool, g_out_d, w_out, w_ff1, w_ff2, g_final):
    batch, seq, d = x.shape
    depth = w_mod.shape[0]
    assert seq % (SB_GROUP * SB_QBLOCK) == 0 and seq % (GDN_INTERLEAVE * CHUNK) == 0 and d == HEADS * WG
    tm = min(512, seq)
    consts = _constants()

    mod = _modulation(c, w_mod, b_mod).reshape(depth, batch, N_MOD, d)
    w_in_p = _pad_w_in(w_in)
    w_out_b, w_ff1_b, w_ff2_b = w_out.astype(BF16), w_ff1.astype(BF16), w_ff2.astype(BF16)
    alog_b, dtb_b = _per_head_lanes(a_log), _per_head_lanes(dt_bias)
    wsg_cat = jnp.transpose(w_sg, (0, 2, 1, 3)).reshape(depth, SG_BLOCK, HEADS * SG_BLOCK)
    bsg_packed = jnp.repeat(jnp.transpose(b_sg, (0, 2, 1)), HD, axis=-1)
    wpool_bd = (jnp.eye(HEADS, dtype=F32)[None, :, None, :, None] * w_pool[:, :, :, None, :]).reshape(
        depth, WG, WG).astype(BF16)
    row = lambda v: v[:, None, :]

    x2 = x.reshape(batch * seq, d)
    for l in range(depth):
        sg = (row(g_ln_b)[l], wsg_cat[l], bsg_packed[l], wpool_bd[l], row(s_pool)[l])
        p, qkv_d, y_b, y_c = _inproj(x2, mod[l], row(g_norm1)[l], w_in_p, l, sg, consts, seq, tm)
        y_a = _gdn(p, conv_a[l], alog_b[l], dtb_b[l], row(g_out_a)[l], consts, batch, seq)
        y_d = _stick_breaking(qkv_d, row(g_out_d)[l], consts, batch, seq)
        x2 = _outmlp(x2, (y_a, y_b, y_c, y_d), mod[l], row(g_norm2)[l], g_final[None, :], w_out_b,
                     w_ff1_b, w_ff2_b, l, seq, tm, final=(l == depth - 1))
    return x2.reshape(batch, seq, d)
```

```python
import functools

import jax
import jax.numpy as jnp
from jax import lax
from jax.experimental import pallas as pl
from jax.experimental.pallas import tpu as pltpu

F32 = jnp.float32
BF16 = jnp.bfloat16
MIX_OUT = BF16

EPS = 1e-6
N_MOD = 6
HEADS = 4
HD = 64
WG = HEADS * HD
CHUNK = 64
CONV_W = 4
SG_BLOCK = 128
SB_QBLOCK = 128
POOL_WINDOWS = (2, 4, 8, 16)
POOL_HALO = 16
TOKEN_SUBTILES = 2
PREP_ROWS = 256
GDN_INTERLEAVE = 8
SB_GROUP = 8
LOG2_E = 1.4426950408889634
SB_ZERO_LOG2 = -160.0

COL_QKV_A, COL_GATE_A, COL_AB, COL_QKV_D, COL_U_B, COL_V_B, COL_X_C = 0, 768, 1024, 1152, 1920, 2176, 2432
AB_PAD = 128
P_GDN = COL_QKV_D
P_SB = COL_U_B
P_PAD = COL_X_C + 256

VMEM_LIMIT = 56 * 1024 * 1024


def _dot(a, b):
    return jnp.dot(a, b, preferred_element_type=F32)


def _dot_nt(a, b):
    return lax.dot_general(a, b, (((1,), (1,)), ((), ())), preferred_element_type=F32)


def _split(x):
    hi = x.astype(BF16)
    lo = (x - hi.astype(F32)).astype(BF16)
    return hi, lo


def _dot_exact_rhs(x, m):
    hi, lo = _split(x)
    return _dot(hi, m) + _dot(lo, m)


def _dot_exact_lhs(m, x):
    hi, lo = _split(x)
    return _dot(m, hi) + _dot(m, lo)


def _sigmoid(x):
    return 1.0 / (1.0 + jnp.exp(-x))


def _softplus(x):
    return jnp.maximum(x, 0.0) + jnp.log(1.0 + jnp.exp(-jnp.abs(x)))


def _lane_head(shape):
    return lax.broadcasted_iota(jnp.int32, shape, 1) // HD


def _blockdiag(y, width):
    lane_blk = lax.broadcasted_iota(jnp.int32, y.shape, 1) // width
    return jnp.concatenate([jnp.where(lane_blk == h, y, 0.0) for h in range(HEADS)], axis=0)


def _shift_rows(x, k):
    return pltpu.roll(x, k, axis=0)


def _mod_kernel(c_ref, w_ref, b_ref, o_ref):
    c = c_ref[...]
    cond = c * _sigmoid(c)
    o_ref[0] = _dot(cond.astype(BF16), w_ref[0].astype(BF16)) + b_ref[0]


def _modulation(c, w_mod, b_mod):
    depth, d, n6 = w_mod.shape
    b = c.shape[0]
    nblk = n6 // d
    return pl.pallas_call(
        _mod_kernel,
        grid=(depth, nblk),
        in_specs=[
            pl.BlockSpec((b, d), lambda l, j: (0, 0)),
            pl.BlockSpec((1, d, d), lambda l, j: (l, 0, j)),
            pl.BlockSpec((1, 1, d), lambda l, j: (l, 0, j)),
        ],
        out_specs=pl.BlockSpec((1, b, d), lambda l, j: (l, 0, j)),
        out_shape=jax.ShapeDtypeStruct((depth, b, n6), F32),
        name="modulation",
    )(c, w_mod, b_mod.reshape(depth, 1, n6))


def _rms(x, g):
    return x * lax.rsqrt(jnp.mean(x * x, axis=-1, keepdims=True) + EPS) * g


def _sg_pool_stages(rows, u, v, xc, halo, t0, gln_ref, wsg_ref, bsg_ref, wpool_ref, spool_ref, gsum, yb_ref,
                    yc_ref):
    blocks = [slice(k * SG_BLOCK, (k + 1) * SG_BLOCK) for k in range(rows // SG_BLOCK)]
    t_chunk = lax.broadcasted_iota(jnp.int32, wsg_ref.shape, 0) // CHUNK
    s_chunk = (lax.broadcasted_iota(jnp.int32, wsg_ref.shape, 1) % SG_BLOCK) // CHUNK
    wm = jnp.where(t_chunk >= s_chunk, wsg_ref[...], 0.0).astype(BF16)
    whole = slice(0, rows)
    mean = _dot(v(whole).astype(BF16), gsum) * (1.0 / HD)
    yield
    vc = v(whole) - mean
    var = _dot((vc * vc).astype(BF16), gsum) * (1.0 / HD)
    lane_grp = _lane_head((PREP_ROWS, WG))
    win = jnp.zeros((PREP_ROWS, WG), jnp.int32)
    for g, w in enumerate(POOL_WINDOWS):
        win = jnp.where(lane_grp == g, w, win)
    pooled = []
    for piece in range(rows // PREP_ROWS):
        cur = xc(slice(piece * PREP_ROWS, (piece + 1) * PREP_ROWS))
        prev = halo if piece == 0 else xc(slice(piece * PREP_ROWS - POOL_HALO, piece * PREP_ROWS))
        s = jnp.concatenate([prev, cur], axis=0)
        total = jnp.zeros((PREP_ROWS, WG), F32)
        k = 1
        for g, w in enumerate(POOL_WINDOWS):
            while k < w:
                s = s + _shift_rows(s, k)
                k *= 2
            total = jnp.where(lane_grp == g, s[POOL_HALO:, :], total)
        t = t0 + piece * PREP_ROWS + lax.broadcasted_iota(jnp.int32, (PREP_ROWS, WG), 0)
        count = jnp.minimum(t + 1, win).astype(F32)
        pooled.append((total / count - cur).astype(BF16))
    yield
    vn = vc * lax.rsqrt(var + EPS) * gln_ref[...]
    vns = [_blockdiag(vn[r, :], HD).astype(BF16) for r in blocks]
    for r, vn in zip(blocks, vns):
        yb_ref[r, :] = (u(r) * (_dot(wm, vn) + bsg_ref[...])).astype(yb_ref.dtype)
    for piece, pl_ in enumerate(pooled):
        yc_ref[piece * PREP_ROWS:(piece + 1) * PREP_ROWS, :] = (
            _dot(pl_, wpool_ref[...]) * spool_ref[...]).astype(yc_ref.dtype)


def _inproj_kernel(x_ref, mod_ref, g_ref, w_ref, gln_ref, wsg_ref, bsg_ref, wpool_ref, spool_ref, gsum_ref,
                   o_ref, osb_ref, yb_ref, yc_ref, uvx_s, halo_s, *, tiles_per_seq):
    i = pl.program_id(0)
    tm = x_ref.shape[0]

    @pl.when(i == 0)
    def _():
        uvx_s[...] = jnp.zeros_like(uvx_s)
        halo_s[...] = jnp.zeros_like(halo_s)

    prev_tile = jnp.maximum(i - 1, 0)
    prev_buf = (i + 1) % 2
    u, v, xc = (functools.partial(lambda k, r: uvx_s[prev_buf, r, k * WG:(k + 1) * WG], k) for k in range(3))
    halo = jnp.where(prev_tile % tiles_per_seq == 0, 0.0, halo_s[...])
    t0 = (prev_tile % tiles_per_seq) * tm
    side = _sg_pool_stages(tm, u, v, xc, halo, t0, gln_ref, wsg_ref, bsg_ref, wpool_ref, spool_ref, gsum_ref[...],
                           yb_ref, yc_ref)

    m = mod_ref[0]
    sub = tm // TOKEN_SUBTILES
    parts = [slice(k * sub, (k + 1) * sub) for k in range(TOKEN_SUBTILES)]
    hs = [(_rms(x_ref[r, :], g_ref[...]) * (1.0 + m[1:2]) + m[0:1]).astype(BF16) for r in parts]
    next(side)
    new_halo = xc(slice(tm - POOL_HALO, tm))
    for r, h in zip(parts, hs):
        p = _dot(h, w_ref[...])
        o_ref[r, :] = p[:, :P_GDN]
        osb_ref[r, :] = p[:, P_GDN:P_SB].astype(osb_ref.dtype)
        uvx_s[i % 2, r, :] = p[:, P_SB:]
        next(side, None)
    for _ in side:
        pass
    halo_s[...] = new_halo


def _inproj(x2, mod_l, g1, w_in_p, layer, sg, consts, seq, tm):
    n, d = x2.shape
    tiles_per_seq = seq // tm
    n_tiles = n // tm
    cur = lambda i: jnp.minimum(i, n_tiles - 1)
    full = lambda a: pl.BlockSpec(a.shape, lambda i: (0,) * a.ndim)
    mix_out = pl.BlockSpec((tm, WG), lambda i: (jnp.maximum(i - 1, 0), 0))
    return pl.pallas_call(
        functools.partial(_inproj_kernel, tiles_per_seq=tiles_per_seq),
        grid=(n_tiles + 1,),
        in_specs=[
            pl.BlockSpec((tm, d), lambda i: (cur(i), 0)),
            pl.BlockSpec((1, N_MOD, d), lambda i: (cur(i) // tiles_per_seq, 0, 0)),
            pl.BlockSpec((1, d), lambda i: (0, 0)),
            pl.BlockSpec((None, d, P_PAD), lambda i: (layer, 0, 0)),
        ] + [full(a) for a in sg] + [full(consts["gsum"])],
        out_specs=[pl.BlockSpec((tm, P_GDN), lambda i: (cur(i), 0)),
                   pl.BlockSpec((tm, P_SB - P_GDN), lambda i: (cur(i), 0)), mix_out, mix_out],
        out_shape=[jax.ShapeDtypeStruct((n, P_GDN), F32), jax.ShapeDtypeStruct((n, P_SB - P_GDN), BF16)]
        + [jax.ShapeDtypeStruct((n, WG), MIX_OUT)] * 2,
        scratch_shapes=[pltpu.VMEM((2, tm, 3 * WG), F32), pltpu.VMEM((POOL_HALO, WG), F32)],
        compiler_params=pltpu.CompilerParams(dimension_semantics=("arbitrary",), vmem_limit_bytes=VMEM_LIMIT),
        name="inproj_gating_pool",
    )(x2, mod_l, g1, w_in_p, *sg, consts["gsum"])


def _gdn_kernel(qkv_ref, ab_ref, gate_ref, convw_ref, alog_ref, dtb_ref, gout_ref,
                gsum_ref, expand_ref, ltri_ref, o_ref,
                q_s, k_s, v_s, beta_s, gc_s, o_s, u_s, wq_s, qk_s, kd_s, state_s):
    seq = qkv_ref.shape[0]
    gsum = gsum_ref[...]

    def prep(r, _):
        r0 = pl.multiple_of(r * PREP_ROWS, PREP_ROWS)
        cur = qkv_ref[pl.ds(r0, PREP_ROWS), :]
        prev = qkv_ref[pl.ds(pl.multiple_of(jnp.maximum(r0 - 8, 0), 8), 8), :]
        prev = jnp.where(r > 0, prev, 0.0)
        ext = jnp.concatenate([prev, cur], axis=0)
        acc = cur * convw_ref[CONV_W - 1:CONV_W, :]
        for k in range(1, CONV_W):
            acc = acc + _shift_rows(ext, k)[8:, :] * convw_ref[CONV_W - 1 - k:CONV_W - k, :]
        y = acc * _sigmoid(acc)
        q, kk, v = y[:, :WG], y[:, WG:2 * WG], y[:, 2 * WG:]
        q = q * lax.rsqrt(_dot((q * q).astype(BF16), gsum) + EPS) * (HD ** -0.5)
        kk = kk * lax.rsqrt(_dot((kk * kk).astype(BF16), gsum) + EPS)
        abx = _dot_exact_rhs(ab_ref[pl.ds(r0, PREP_ROWS), :], expand_ref[...])
        a_b, b_b = abx[:, :WG], abx[:, WG:]
        g = -jnp.exp(alog_ref[...]) * _softplus(a_b + dtb_ref[...])
        rows = pl.ds(r0, PREP_ROWS)
        q_s[rows, :] = q
        k_s[rows, :] = kk
        v_s[rows, :] = v
        beta_s[rows, :] = _sigmoid(b_b)
        gc_s[rows, :] = _dot_exact_lhs(ltri_ref[...], g)
        return 0

    lax.fori_loop(0, seq // PREP_ROWS, prep, 0)

    state_s[...] = jnp.zeros_like(state_s)
    shape = (CHUNK, WG)
    row = lax.broadcasted_iota(jnp.int32, shape, 0)
    col = lax.broadcasted_iota(jnp.int32, shape, 1) % CHUNK
    eye = row == col
    incl = row >= col
    strict = row > col

    def bd(y):
        return _blockdiag(y, HD).astype(BF16)

    def intra_stages(i):
        chunks = [i * GDN_INTERLEAVE + k for k in range(GDN_INTERLEAVE)]
        rows = [pl.ds(pl.multiple_of(c * CHUNK, CHUNK), CHUNK) for c in chunks]
        lows, ts = [], []
        for c, r in zip(chunks, rows):
            qn, kn, gc = q_s[r, :], k_s[r, :], gc_s[r, :]
            kb = kn * beta_s[r, :]
            a = _dot_nt(jnp.concatenate([kb, qn], axis=0).astype(BF16), bd(kn))
            yield
            gc_row = jnp.sum(jnp.where(eye, gc, 0.0), axis=0, keepdims=True)
            decay = jnp.where(incl, jnp.exp(gc - gc_row), 0.0)
            low = jnp.where(strict, a[:CHUNK] * decay, 0.0)
            qk_s[r, :] = (a[CHUNK:] * decay).astype(BF16)
            wq_s[pl.ds(pl.multiple_of(c * 2 * CHUNK + CHUNK, CHUNK), CHUNK), :] = (qn * jnp.exp(gc)).astype(BF16)
            kd_t = (kn * jnp.exp(gc[CHUNK - 1:CHUNK, :] - gc)).T
            kd_s[r, :] = jnp.concatenate([kd_t[h * HD:(h + 1) * HD, :] for h in range(HEADS)],
                                         axis=1).astype(BF16)
            lows.append(low)
            ts.append(jnp.where(eye, 1.0, 0.0) - jnp.where(row // 2 == col // 2, low, 0.0))
        s = 2
        while s < CHUNK:
            join = jnp.logical_and(row // (2 * s) == col // (2 * s), row // s != col // s)
            tes = []
            for t, low in zip(ts, lows):
                tes.append(_dot(t.astype(BF16), bd(jnp.where(join, low, 0.0))))
                yield
            for k, te in enumerate(tes):
                ts[k] = ts[k] - _dot(te.astype(BF16), bd(ts[k]))
                yield
            s *= 2
        for c, r, t in zip(chunks, rows, ts):
            beta, gc = beta_s[r, :], gc_s[r, :]
            rhs = jnp.concatenate([bd(v_s[r, :] * beta), bd(k_s[r, :] * beta * jnp.exp(gc))], axis=1)
            uw = _dot(t.astype(BF16), rhs)
            yield
            u_s[r, :] = uw[:, :WG]
            wq_s[pl.ds(pl.multiple_of(c * 2 * CHUNK, 2 * CHUNK), CHUNK), :] = uw[:, WG:].astype(BF16)

    def scan_stages(first, count):
        state = state_s[...]
        for k in range(count):
            c = first + k
            rows = pl.ds(pl.multiple_of(c * CHUNK, CHUNK), CHUNK)
            wq = _dot(wq_s[pl.ds(pl.multiple_of(c * 2 * CHUNK, 2 * CHUNK), 2 * CHUNK), :], bd(state))
            yield
            v_new = u_s[rows, :] - wq[:CHUNK]
            ou = _dot(jnp.concatenate([qk_s[rows, :], kd_s[rows, :]], axis=0), bd(v_new))
            yield
            o_s[rows, :] = wq[CHUNK:] + ou[:CHUNK]
            gl = gc_s[pl.ds(pl.multiple_of(c * CHUNK + CHUNK - 8, 8), 8), :][7:8, :]
            state = state * jnp.exp(gl) + ou[CHUNK:]
        state_s[...] = state

    def drain(stages):
        for _ in stages:
            pass

    def interleave(main, side, main_per_side):
        main_live = side_live = True
        while main_live or side_live:
            if side_live:
                side_live = next(side, 0) is None
            for _ in range(main_per_side):
                if main_live:
                    main_live = next(main, 0) is None

    n_groups = seq // (CHUNK * GDN_INTERLEAVE)
    intra_matmuls = GDN_INTERLEAVE * (2 + 2 * (CHUNK.bit_length() - 2))
    scan_matmuls = GDN_INTERLEAVE * 2

    drain(intra_stages(0))

    def merged(g, _):
        interleave(intra_stages(g), scan_stages((g - 1) * GDN_INTERLEAVE, GDN_INTERLEAVE),
                   intra_matmuls // scan_matmuls)
        return 0

    lax.fori_loop(1, n_groups, merged, 0)

    def post(r):
        rows = pl.ds(pl.multiple_of(r * PREP_ROWS, PREP_ROWS), PREP_ROWS)
        o = o_s[rows, :]
        ms = _dot((o * o).astype(BF16), gsum) * (1.0 / HD)
        gate = gate_ref[rows, :]
        y = o * lax.rsqrt(ms + EPS) * gout_ref[...] * (gate * _sigmoid(gate))
        o_ref[rows, :] = y.astype(o_ref.dtype)

    n_tiles = seq // PREP_ROWS
    done_tiles = (n_groups - 1) * GDN_INTERLEAVE * CHUNK // PREP_ROWS
    overlapped = min(done_tiles, GDN_INTERLEAVE)
    last_first = (n_groups - 1) * GDN_INTERLEAVE

    def last_scan_with_post(i, _):
        post(i)
        drain(scan_stages(last_first + i, 1))
        return 0

    def last_scan(c, _):
        drain(scan_stages(c, 1))
        return 0

    def post_only(r, _):
        post(r)
        return 0

    lax.fori_loop(0, overlapped, last_scan_with_post, 0)
    lax.fori_loop(last_first + overlapped, last_first + GDN_INTERLEAVE, last_scan, 0)
    lax.fori_loop(overlapped, n_tiles, post_only, 0)


def _gdn(p, conv_w, alog_b, dtb_b, gout, consts, batch, seq):
    n = batch * seq
    vec = lambda w: pl.BlockSpec((1, w), lambda b: (0, 0))
    full = lambda a: pl.BlockSpec(a.shape, lambda b: (0,) * a.ndim)
    return pl.pallas_call(
        _gdn_kernel,
        grid=(batch,),
        in_specs=[
            pl.BlockSpec((seq, 3 * WG), lambda b: (b, COL_QKV_A // (3 * WG))),
            pl.BlockSpec((seq, AB_PAD), lambda b: (b, COL_AB // AB_PAD)),
            pl.BlockSpec((seq, WG), lambda b: (b, COL_GATE_A // WG)),
            full(conv_w), vec(WG), vec(WG), vec(WG),
            full(consts["gsum"]), full(consts["expand"]), full(consts["ltri"]),
        ],
        out_specs=pl.BlockSpec((seq, WG), lambda b: (b, 0)),
        out_shape=jax.ShapeDtypeStruct((n, WG), MIX_OUT),
        scratch_shapes=[pltpu.VMEM((seq, WG), F32)] * 7 + [
            pltpu.VMEM((2 * seq, WG), BF16), pltpu.VMEM((seq, WG), BF16), pltpu.VMEM((seq, WG), BF16),
            pltpu.VMEM((CHUNK, WG), F32)],
        compiler_params=pltpu.CompilerParams(vmem_limit_bytes=VMEM_LIMIT),
        name="gated_deltanet",
    )(p, p, p, conv_w, alog_b, dtb_b, gout, consts["gsum"], consts["expand"], consts["ltri"])


def _sb_kernel(qkv_ref, gout_ref, gsum_ref, sufsum_ref, o_ref, kbd_s, vbd_s, acc_s, carry_s):
    seq = qkv_ref.shape[0]
    gi = pl.program_id(1)
    blk = SB_QBLOCK

    @pl.when(gi == 0)
    def _():
        def build(j, _):
            rows = pl.ds(pl.multiple_of(j * blk, blk), blk)
            kt = qkv_ref[rows, WG:2 * WG].astype(F32).T
            row_head = lax.broadcasted_iota(jnp.int32, kt.shape, 0) // HD
            kbd_s[j] = jnp.concatenate([jnp.where(row_head == h, kt, 0.0) for h in range(HEADS)],
                                       axis=1).astype(BF16)
            vbd_s[j] = _blockdiag(qkv_ref[rows, 2 * WG:].astype(F32), HD).astype(BF16)
            return 0
        lax.fori_loop(0, seq // blk, build, 0)

    sufsum = sufsum_ref[...]
    past = lax.broadcasted_iota(jnp.int32, (blk, blk), 1) < lax.broadcasted_iota(jnp.int32, (blk, blk), 0)
    qblocks = [gi * SB_GROUP + k for k in range(SB_GROUP)]
    qs = [(qkv_ref[pl.ds(pl.multiple_of(qb * blk, blk), blk), :WG].astype(F32) * (LOG2_E * HD ** -0.5)).astype(BF16)
          for qb in qblocks]

    def step(offset, diagonal):
        js = [jnp.maximum(qb - offset, 0) for qb in qblocks]
        zs = [_dot(q, kbd_s[j]) for q, j in zip(qs, js)]
        probs = []
        for k, z in enumerate(zs):
            rows = slice(k * blk, (k + 1) * blk)
            lss, nlfs = [], []
            for h in range(HEADS):
                zh = z[:, h * blk:(h + 1) * blk]
                nlf = jnp.maximum(zh, 0.0) + jnp.log2(1.0 + jnp.exp2(-jnp.abs(zh)))
                lss.append(zh - nlf)
                nlfs.append((jnp.where(past, nlf, 0.0) if diagonal else nlf).astype(BF16))
            cs = _dot(jnp.concatenate(nlfs, axis=0), sufsum)
            per_head = []
            for h in range(HEADS):
                lanes = slice(h * blk, (h + 1) * blk)
                csh = cs[h * blk:(h + 1) * blk, :]
                carry = carry_s[rows, lanes]
                a = jnp.exp2(lss[h] + csh[:, :blk] + carry)
                if diagonal:
                    a = jnp.where(past, a, 0.0)
                carry_s[rows, lanes] = carry + csh[:, blk:]
                per_head.append(a.astype(BF16))
            probs.append(jnp.concatenate(per_head, axis=1))
        for k, (p, j) in enumerate(zip(probs, js)):
            rows = slice(k * blk, (k + 1) * blk)
            av = _dot(p, vbd_s[j])
            if not diagonal:
                av = jnp.where(qblocks[k] >= offset, av, 0.0)
            acc_s[rows, :] += av

    def largest_live_carry(offset):
        live = jnp.float32(-jnp.inf)
        for k, qb in enumerate(qblocks):
            live = jnp.maximum(live, jnp.where(qb >= offset, jnp.max(carry_s[k * blk:(k + 1) * blk, :]), -jnp.inf))
        return live

    acc_s[...] = jnp.zeros_like(acc_s)
    carry_s[...] = jnp.zeros_like(carry_s)
    step(0, True)

    def cond(state):
        offset, largest = state
        return jnp.logical_and(offset <= qblocks[-1], largest > SB_ZERO_LOG2)

    def body(state):
        offset, _ = state
        step(offset, False)
        return offset + 1, largest_live_carry(offset + 1)

    lax.while_loop(cond, body, (jnp.int32(1), largest_live_carry(1)))

    o = acc_s[...]
    ms = _dot((o * o).astype(BF16), gsum_ref[...]) * (1.0 / HD)
    o_ref[...] = (o * lax.rsqrt(ms + EPS) * gout_ref[...]).astype(o_ref.dtype)


def _stick_breaking(p, gout, consts, batch, seq):
    n = batch * seq
    nq = seq // SB_QBLOCK
    ng = nq // SB_GROUP
    rows = SB_GROUP * SB_QBLOCK
    full = lambda a: pl.BlockSpec(a.shape, lambda b, i: (0,) * a.ndim)
    return pl.pallas_call(
        _sb_kernel,
        grid=(batch, ng),
        in_specs=[
            pl.BlockSpec((seq, 3 * WG), lambda b, i: (b, 0)),
            pl.BlockSpec((1, WG), lambda b, i: (0, 0)),
            full(consts["gsum"]), full(consts["sufsum"]),
        ],
        out_specs=pl.BlockSpec((rows, WG), lambda b, i: (b * ng + i, 0)),
        out_shape=jax.ShapeDtypeStruct((n, WG), MIX_OUT),
        scratch_shapes=[
            pltpu.VMEM((nq, WG, HEADS * SB_QBLOCK), BF16),
            pltpu.VMEM((nq, HEADS * SB_QBLOCK, WG), BF16),
            pltpu.VMEM((rows, WG), F32),
            pltpu.VMEM((rows, HEADS * SB_QBLOCK), F32),
        ],
        compiler_params=pltpu.CompilerParams(
            dimension_semantics=("arbitrary", "arbitrary"), vmem_limit_bytes=VMEM_LIMIT),
        name="stick_breaking",
    )(p, gout, consts["gsum"], consts["sufsum"])


def _outmlp_kernel(x_ref, ya_ref, yb_ref, yc_ref, yd_ref, mod_ref, g2_ref, gfin_ref,
                   wout_ref, w1_ref, w2_ref, o_ref, *, ff_chunk, final):
    m = mod_ref[0]
    sub = x_ref.shape[0] // TOKEN_SUBTILES
    parts = [slice(k * sub, (k + 1) * sub) for k in range(TOKEN_SUBTILES)]
    ys = [jnp.concatenate([ref[r, :].astype(BF16) for ref in (ya_ref, yb_ref, yc_ref, yd_ref)], axis=1)
          for r in parts]
    xs = [x_ref[r, :] + m[2:3] * _dot(y, wout_ref[...]) for r, y in zip(parts, ys)]
    hs = [(_rms(x, g2_ref[...]) * (1.0 + m[4:5]) + m[3:4]).astype(BF16) for x in xs]
    d_ff = w1_ref.shape[1]
    ffs = [jnp.zeros(x.shape, F32) for x in xs]
    for k in range(0, d_ff, ff_chunk):
        acts = [jnp.maximum(_dot(h, w1_ref[:, k:k + ff_chunk]), 0.0) for h in hs]
        ffs = [ff + _dot((a * a).astype(BF16), w2_ref[k:k + ff_chunk, :]) for ff, a in zip(ffs, acts)]
    for r, x, ff in zip(parts, xs, ffs):
        x = x + m[5:6] * ff
        if final:
            x = _rms(x, gfin_ref[...])
        o_ref[r, :] = x


def _outmlp(x2, ys, mod_l, g2, gfin, w_out, w_ff1, w_ff2, layer, seq, tm, final):
    n, d = x2.shape
    d_ff = w_ff1.shape[-1]
    tiles_per_seq = seq // tm
    tile = lambda w: pl.BlockSpec((tm, w), lambda i: (i, 0))
    const = lambda a: pl.BlockSpec((None,) + a.shape[1:], lambda i: (layer, 0, 0), pipeline_mode=pl.Buffered(1))
    return pl.pallas_call(
        functools.partial(_outmlp_kernel, ff_chunk=min(d_ff, 1024), final=final),
        grid=(n // tm,),
        in_specs=[
            tile(d), tile(WG), tile(WG), tile(WG), tile(WG),
            pl.BlockSpec((1, N_MOD, d), lambda i: (i // tiles_per_seq, 0, 0)),
            pl.BlockSpec((1, d), lambda i: (0, 0)),
            pl.BlockSpec((1, d), lambda i: (0, 0)),
            const(w_out), const(w_ff1), const(w_ff2),
        ],
        out_specs=tile(d),
        out_shape=jax.ShapeDtypeStruct((n, d), F32),
        compiler_params=pltpu.CompilerParams(vmem_limit_bytes=VMEM_LIMIT),
        name="outproj_mlp",
    )(x2, *ys, mod_l, g2, gfin, w_out, w_ff1, w_ff2)


def _constants():
    idx = jnp.arange(WG)
    gsum = (idx[:, None] // HD == idx[None, :] // HD).astype(BF16)
    ab = jnp.arange(AB_PAD)[:, None]
    lane = jnp.arange(2 * WG)[None, :]
    expand = (ab == (lane // WG) * HEADS + (lane % WG) // HD).astype(BF16)
    r = jnp.arange(PREP_ROWS)
    ltri = ((r[:, None] // CHUNK == r[None, :] // CHUNK) & (r[:, None] >= r[None, :])).astype(BF16)
    k = jnp.arange(SB_QBLOCK)
    suf = (k[:, None] > k[None, :]).astype(BF16)
    sufsum = -jnp.concatenate([suf, jnp.ones((SB_QBLOCK, SB_QBLOCK), BF16)], axis=1)
    return {"gsum": gsum, "expand": expand, "ltri": ltri, "sufsum": sufsum}


def _pad_w_in(w_in):
    o = 0
    parts = {}
    for name, size in (("qkv_a", 3 * WG), ("ab", 2 * HEADS), ("gate_a", WG), ("u_b", WG), ("v_b", WG),
                       ("x_c", WG), ("qkv_d", 3 * WG)):
        parts[name] = w_in[..., o:o + size]
        o += size
    ab = jnp.pad(parts["ab"], ((0, 0), (0, 0), (0, AB_PAD - 2 * HEADS)))
    return jnp.concatenate([parts["qkv_a"], parts["gate_a"], ab, parts["qkv_d"], parts["u_b"], parts["v_b"],
                            parts["x_c"]], axis=-1).astype(BF16)


def _per_head_lanes(v):
    return jnp.repeat(v, HD, axis=-1)[..., None, :]


def kernel(x, c, w_mod, b_mod, g_norm1, g_norm2, w_in, conv_a, a_log, dt_bias, g_out_a, g_ln_b, w_sg, b_sg,
           w_pool, s_pool, g_out_d, w_out, w_ff1, w_ff2, g_final):
    batch, seq, d = x.shape
    depth = w_mod.shape[0]
    assert seq % (SB_GROUP * SB_QBLOCK) == 0 and seq % (GDN_INTERLEAVE * CHUNK) == 0 and d == HEADS * WG
    tm = min(512, seq)
    consts = _constants()

    mod = _modulation(c, w_mod, b_mod).reshape(depth, batch, N_MOD, d)
    w_in_p = _pad_w_in(w_in)
    w_out_b, w_ff1_b, w_ff2_b = w_out.astype(BF16), w_ff1.astype(BF16), w_ff2.astype(BF16)
    alog_b, dtb_b = _per_head_lanes(a_log), _per_head_lanes(dt_bias)
    wsg_cat = jnp.transpose(w_sg, (0, 2, 1, 3)).reshape(depth, SG_BLOCK, HEADS * SG_BLOCK)
    bsg_packed = jnp.repeat(jnp.transpose(b_sg, (0, 2, 1)), HD, axis=-1)
    wpool_bd = (jnp.eye(HEADS, dtype=F32)[None, :, None, :, None] * w_pool[:, :, :, None, :]).reshape(
        depth, WG, WG).astype(BF16)
    row = lambda v: v[:, None, :]

    x2 = x.reshape(batch * seq, d)
    for l in range(depth):
        sg = (row(g_ln_b)[l], wsg_cat[l], bsg_packed[l], wpool_bd[l], row(s_pool)[l])
        p, qkv_d, y_b, y_c = _inproj(x2, mod[l], row(g_norm1)[l], w_in_p, l, sg, consts, seq, tm)
        y_a = _gdn(p, conv_a[l], alog_b[l], dtb_b[l], row(g_out_a)[l], consts, batch, seq)
        y_d = _stick_breaking(qkv_d, row(g_out_d)[l], consts, batch, seq)
        x2 = _outmlp(x2, (y_a, y_b, y_c, y_d), mod[l], row(g_norm2)[l], g_final[None, :], w_out_b,
                     w_ff1_b, w_ff2_b, l, seq, tm, final=(l == depth - 1))
    return x2.reshape(batch, seq, d)
```

```python
import functools

import jax
import jax.numpy as jnp
from jax import lax
from jax.experimental import pallas as pl
from jax.experimental.pallas import tpu as pltpu

F32 = jnp.float32
BF16 = jnp.bfloat16
MIX_OUT = BF16

EPS = 1e-6
N_MOD = 6
HEADS = 4
HD = 64
WG = HEADS * HD
CHUNK = 64
CONV_W = 4
SG_BLOCK = 128
SB_QBLOCK = 128
POOL_WINDOWS = (2, 4, 8, 16)
POOL_HALO = 16
TOKEN_SUBTILES = 2
PREP_ROWS = 256
GDN_INTERLEAVE = 8
SB_GROUP = 8
LOG2_E = 1.4426950408889634
SB_ZERO_LOG2 = -160.0

COL_QKV_A, COL_GATE_A, COL_AB, COL_QKV_D, COL_U_B, COL_V_B, COL_X_C = 0, 768, 1024, 1152, 1920, 2176, 2432
AB_PAD = 128
P_GDN = COL_QKV_D
P_SB = COL_U_B
P_PAD = COL_X_C + 256

VMEM_LIMIT = 56 * 1024 * 1024


def _dot(a, b):
    return jnp.dot(a, b, preferred_element_type=F32)


def _dot_nt(a, b):
    return lax.dot_general(a, b, (((1,), (1,)), ((), ())), preferred_element_type=F32)


def _split(x):
    hi = x.astype(BF16)
    lo = (x - hi.astype(F32)).astype(BF16)
    return hi, lo


def _dot_exact_rhs(x, m):
    hi, lo = _split(x)
    return _dot(hi, m) + _dot(lo, m)


def _dot_exact_lhs(m, x):
    hi, lo = _split(x)
    return _dot(m, hi) + _dot(m, lo)


def _sigmoid(x):
    return 1.0 / (1.0 + jnp.exp(-x))


def _softplus(x):
    return jnp.maximum(x, 0.0) + jnp.log(1.0 + jnp.exp(-jnp.abs(x)))


def _lane_head(shape):
    return lax.broadcasted_iota(jnp.int32, shape, 1) // HD


def _blockdiag(y, width):
    lane_blk = lax.broadcasted_iota(jnp.int32, y.shape, 1) // width
    return jnp.concatenate([jnp.where(lane_blk == h, y, 0.0) for h in range(HEADS)], axis=0)


def _shift_rows(x, k):
    return pltpu.roll(x, k, axis=0)


def _mod_kernel(c_ref, w_ref, b_ref, o_ref):
    c = c_ref[...]
    cond = c * _sigmoid(c)
    o_ref[0] = _dot(cond.astype(BF16), w_ref[0].astype(BF16)) + b_ref[0]


def _modulation(c, w_mod, b_mod):
    depth, d, n6 = w_mod.shape
    b = c.shape[0]
    nblk = n6 // d
    return pl.pallas_call(
        _mod_kernel,
        grid=(depth, nblk),
        in_specs=[
            pl.BlockSpec((b, d), lambda l, j: (0, 0)),
            pl.BlockSpec((1, d, d), lambda l, j: (l, 0, j)),
            pl.BlockSpec((1, 1, d), lambda l, j: (l, 0, j)),
        ],
        out_specs=pl.BlockSpec((1, b, d), lambda l, j: (l, 0, j)),
        out_shape=jax.ShapeDtypeStruct((depth, b, n6), F32),
        name="modulation",
    )(c, w_mod, b_mod.reshape(depth, 1, n6))


def _rms(x, g):
    return x * lax.rsqrt(jnp.mean(x * x, axis=-1, keepdims=True) + EPS) * g


def _sg_pool_stages(rows, u, v, xc, halo, t0, gln_ref, wsg_ref, bsg_ref, wpool_ref, spool_ref, gsum, yb_ref,
                    yc_ref):
    blocks = [slice(k * SG_BLOCK, (k + 1) * SG_BLOCK) for k in range(rows // SG_BLOCK)]
    t_chunk = lax.broadcasted_iota(jnp.int32, wsg_ref.shape, 0) // CHUNK
    s_chunk = (lax.broadcasted_iota(jnp.int32, wsg_ref.shape, 1) % SG_BLOCK) // CHUNK
    wm = jnp.where(t_chunk >= s_chunk, wsg_ref[...], 0.0).astype(BF16)
    whole = slice(0, rows)
    mean = _dot(v(whole).astype(BF16), gsum) * (1.0 / HD)
    yield
    vc = v(whole) - mean
    var = _dot((vc * vc).astype(BF16), gsum) * (1.0 / HD)
    lane_grp = _lane_head((PREP_ROWS, WG))
    win = jnp.zeros((PREP_ROWS, WG), jnp.int32)
    for g, w in enumerate(POOL_WINDOWS):
        win = jnp.where(lane_grp == g, w, win)
    pooled = []
    for piece in range(rows // PREP_ROWS):
        cur = xc(slice(piece * PREP_ROWS, (piece + 1) * PREP_ROWS))
        prev = halo if piece == 0 else xc(slice(piece * PREP_ROWS - POOL_HALO, piece * PREP_ROWS))
        s = jnp.concatenate([prev, cur], axis=0)
        total = jnp.zeros((PREP_ROWS, WG), F32)
        k = 1
        for g, w in enumerate(POOL_WINDOWS):
            while k < w:
                s = s + _shift_rows(s, k)
                k *= 2
            total = jnp.where(lane_grp == g, s[POOL_HALO:, :], total)
        t = t0 + piece * PREP_ROWS + lax.broadcasted_iota(jnp.int32, (PREP_ROWS, WG), 0)
        count = jnp.minimum(t + 1, win).astype(F32)
        pooled.append((total / count - cur).astype(BF16))
    yield
    vn = vc * lax.rsqrt(var + EPS) * gln_ref[...]
    vns = [_blockdiag(vn[r, :], HD).astype(BF16) for r in blocks]
    for r, vn in zip(blocks, vns):
        yb_ref[r, :] = (u(r) * (_dot(wm, vn) + bsg_ref[...])).astype(yb_ref.dtype)
    for piece, pl_ in enumerate(pooled):
        yc_ref[piece * PREP_ROWS:(piece + 1) * PREP_ROWS, :] = (
            _dot(pl_, wpool_ref[...]) * spool_ref[...]).astype(yc_ref.dtype)


def _inproj_kernel(x_ref, mod_ref, g_ref, w_ref, gln_ref, wsg_ref, bsg_ref, wpool_ref, spool_ref, gsum_ref,
                   o_ref, osb_ref, yb_ref, yc_ref, uvx_s, halo_s, *, tiles_per_seq):
    i = pl.program_id(0)
    tm = x_ref.shape[0]

    @pl.when(i == 0)
    def _():
        uvx_s[...] = jnp.zeros_like(uvx_s)
        halo_s[...] = jnp.zeros_like(halo_s)

    prev_tile = jnp.maximum(i - 1, 0)
    prev_buf = (i + 1) % 2
    u, v, xc = (functools.partial(lambda k, r: uvx_s[prev_buf, r, k * WG:(k + 1) * WG], k) for k in range(3))
    halo = jnp.where(prev_tile % tiles_per_seq == 0, 0.0, halo_s[...])
    t0 = (prev_tile % tiles_per_seq) * tm
    side = _sg_pool_stages(tm, u, v, xc, halo, t0, gln_ref, wsg_ref, bsg_ref, wpool_ref, spool_ref, gsum_ref[...],
                           yb_ref, yc_ref)

    m = mod_ref[0]
    sub = tm // TOKEN_SUBTILES
    parts = [slice(k * sub, (k + 1) * sub) for k in range(TOKEN_SUBTILES)]
    hs = [(_rms(x_ref[r, :], g_ref[...]) * (1.0 + m[1:2]) + m[0:1]).astype(BF16) for r in parts]
    next(side)
    new_halo = xc(slice(tm - POOL_HALO, tm))
    for r, h in zip(parts, hs):
        p = _dot(h, w_ref[...])
        o_ref[r, :] = p[:, :P_GDN]
        osb_ref[r, :] = p[:, P_GDN:P_SB].astype(osb_ref.dtype)
        uvx_s[i % 2, r, :] = p[:, P_SB:]
        next(side, None)
    for _ in side:
        pass
    halo_s[...] = new_halo


def _inproj(x2, mod_l, g1, w_in_p, layer, sg, consts, seq, tm):
    n, d = x2.shape
    tiles_per_seq = seq // tm
    n_tiles = n // tm
    cur = lambda i: jnp.minimum(i, n_tiles - 1)
    full = lambda a: pl.BlockSpec(a.shape, lambda i: (0,) * a.ndim)
    mix_out = pl.BlockSpec((tm, WG), lambda i: (jnp.maximum(i - 1, 0), 0))
    return pl.pallas_call(
        functools.partial(_inproj_kernel, tiles_per_seq=tiles_per_seq),
        grid=(n_tiles + 1,),
        in_specs=[
            pl.BlockSpec((tm, d), lambda i: (cur(i), 0)),
            pl.BlockSpec((1, N_MOD, d), lambda i: (cur(i) // tiles_per_seq, 0, 0)),
            pl.BlockSpec((1, d), lambda i: (0, 0)),
            pl.BlockSpec((None, d, P_PAD), lambda i: (layer, 0, 0)),
        ] + [full(a) for a in sg] + [full(consts["gsum"])],
        out_specs=[pl.BlockSpec((tm, P_GDN), lambda i: (cur(i), 0)),
                   pl.BlockSpec((tm, P_SB - P_GDN), lambda i: (cur(i), 0)), mix_out, mix_out],
        out_shape=[jax.ShapeDtypeStruct((n, P_GDN), F32), jax.ShapeDtypeStruct((n, P_SB - P_GDN), BF16)]
        + [jax.ShapeDtypeStruct((n, WG), MIX_OUT)] * 2,
        scratch_shapes=[pltpu.VMEM((2, tm, 3 * WG), F32), pltpu.VMEM((POOL_HALO, WG), F32)],
        compiler_params=pltpu.CompilerParams(dimension_semantics=("arbitrary",), vmem_limit_bytes=VMEM_LIMIT),
        name="inproj_gating_pool",
    )(x2, mod_l, g1, w_in_p, *sg, consts["gsum"])


def _gdn_kernel(qkv_ref, ab_ref, gate_ref, convw_ref, alog_ref, dtb_ref, gout_ref,
                gsum_ref, expand_ref, ltri_ref, o_ref,
                q_s, k_s, v_s, beta_s, gc_s, o_s, kdu_s, lhs_s, qk_s, kd_s, state_s):
    seq = qkv_ref.shape[0]
    gsum = gsum_ref[...]

    def prep(r, _):
        r0 = pl.multiple_of(r * PREP_ROWS, PREP_ROWS)
        cur = qkv_ref[pl.ds(r0, PREP_ROWS), :]
        prev = qkv_ref[pl.ds(pl.multiple_of(jnp.maximum(r0 - 8, 0), 8), 8), :]
        prev = jnp.where(r > 0, prev, 0.0)
        ext = jnp.concatenate([prev, cur], axis=0)
        acc = cur * convw_ref[CONV_W - 1:CONV_W, :]
        for k in range(1, CONV_W):
            acc = acc + _shift_rows(ext, k)[8:, :] * convw_ref[CONV_W - 1 - k:CONV_W - k, :]
        y = acc * _sigmoid(acc)
        q, kk, v = y[:, :WG], y[:, WG:2 * WG], y[:, 2 * WG:]
        q = q * lax.rsqrt(_dot((q * q).astype(BF16), gsum) + EPS) * (HD ** -0.5)
        kk = kk * lax.rsqrt(_dot((kk * kk).astype(BF16), gsum) + EPS)
        abx = _dot_exact_rhs(ab_ref[pl.ds(r0, PREP_ROWS), :], expand_ref[...])
        a_b, b_b = abx[:, :WG], abx[:, WG:]
        g = -jnp.exp(alog_ref[...]) * _softplus(a_b + dtb_ref[...])
        rows = pl.ds(r0, PREP_ROWS)
        q_s[rows, :] = q
        k_s[rows, :] = kk
        v_s[rows, :] = v
        beta_s[rows, :] = _sigmoid(b_b)
        gc_s[rows, :] = _dot_exact_lhs(ltri_ref[...], g)
        return 0

    lax.fori_loop(0, seq // PREP_ROWS, prep, 0)

    state_s[...] = jnp.zeros_like(state_s)
    shape = (CHUNK, WG)
    row = lax.broadcasted_iota(jnp.int32, shape, 0)
    col = lax.broadcasted_iota(jnp.int32, shape, 1) % CHUNK
    eye = row == col
    incl = row >= col
    strict = row > col

    def bd(y):
        return _blockdiag(y, HD).astype(BF16)

    def intra_stages(i):
        chunks = [i * GDN_INTERLEAVE + k for k in range(GDN_INTERLEAVE)]
        rows = [pl.ds(pl.multiple_of(c * CHUNK, CHUNK), CHUNK) for c in chunks]
        lows, ts = [], []
        for c, r in zip(chunks, rows):
            qn, kn, gc = q_s[r, :], k_s[r, :], gc_s[r, :]
            kb = kn * beta_s[r, :]
            a = _dot_nt(jnp.concatenate([kb, qn], axis=0).astype(BF16), bd(kn))
            yield
            gc_row = jnp.sum(jnp.where(eye, gc, 0.0), axis=0, keepdims=True)
            decay = jnp.where(incl, jnp.exp(gc - gc_row), 0.0)
            low = jnp.where(strict, a[:CHUNK] * decay, 0.0)
            qk_s[r, :] = (a[CHUNK:] * decay).astype(BF16)
            kd_t = (kn * jnp.exp(gc[CHUNK - 1:CHUNK, :] - gc)).T
            kd_s[r, :] = jnp.concatenate([kd_t[h * HD:(h + 1) * HD, :] for h in range(HEADS)],
                                         axis=1).astype(BF16)
            lows.append(low)
            ts.append(jnp.where(eye, 1.0, 0.0) - jnp.where(row // 2 == col // 2, low, 0.0))
        s = 2
        while s < CHUNK:
            join = jnp.logical_and(row // (2 * s) == col // (2 * s), row // s != col // s)
            tes = []
            for t, low in zip(ts, lows):
                tes.append(_dot(t.astype(BF16), bd(jnp.where(join, low, 0.0))))
                yield
            for k, te in enumerate(tes):
                ts[k] = ts[k] - _dot(te.astype(BF16), bd(ts[k]))
                yield
            s *= 2
        uws = []
        for r, t in zip(rows, ts):
            beta, gc = beta_s[r, :], gc_s[r, :]
            rhs = jnp.concatenate([bd(v_s[r, :] * beta), bd(k_s[r, :] * beta * jnp.exp(gc))], axis=1)
            uws.append(_dot(t.astype(BF16), rhs))
            yield
        for c, r, uw in zip(chunks, rows, uws):
            u, w = uw[:, :WG], uw[:, WG:]
            tr = _dot(jnp.concatenate([kd_s[r, :], qk_s[r, :]], axis=0), jnp.concatenate([bd(w), bd(u)], axis=1))
            yield
            q_eff = q_s[r, :] * jnp.exp(gc_s[r, :]) - tr[CHUNK:, :WG]
            lhs_s[pl.ds(pl.multiple_of(c * 2 * CHUNK, 2 * CHUNK), CHUNK), :] = q_eff.astype(BF16)
            lhs_s[pl.ds(pl.multiple_of(c * 2 * CHUNK + CHUNK, CHUNK), CHUNK), :] = tr[:CHUNK, :WG].astype(BF16)
            kdu_s[r, :] = tr[:CHUNK, WG:]
            o_s[r, :] = tr[CHUNK:, WG:]

    def scan_stages(first, count):
        state = state_s[...]
        for k in range(count):
            c = first + k
            rows = pl.ds(pl.multiple_of(c * CHUNK, CHUNK), CHUNK)
            ws = _dot(lhs_s[pl.ds(pl.multiple_of(c * 2 * CHUNK, 2 * CHUNK), 2 * CHUNK), :], bd(state))
            yield
            gl = gc_s[pl.ds(pl.multiple_of(c * CHUNK + CHUNK - 8, 8), 8), :][7:8, :]
            state = state * jnp.exp(gl) - ws[CHUNK:] + kdu_s[rows, :]
            o_s[rows, :] = o_s[rows, :] + ws[:CHUNK]
        state_s[...] = state

    def drain(stages):
        for _ in stages:
            pass

    def interleave(main, side, main_per_side):
        main_live = side_live = True
        while main_live or side_live:
            if side_live:
                side_live = next(side, 0) is None
            for _ in range(main_per_side):
                if main_live:
                    main_live = next(main, 0) is None

    n_groups = seq // (CHUNK * GDN_INTERLEAVE)
    intra_matmuls = GDN_INTERLEAVE * (3 + 2 * (CHUNK.bit_length() - 2))
    scan_matmuls = GDN_INTERLEAVE

    drain(intra_stages(0))

    def merged(g, _):
        interleave(intra_stages(g), scan_stages((g - 1) * GDN_INTERLEAVE, GDN_INTERLEAVE),
                   intra_matmuls // scan_matmuls)
        return 0

    lax.fori_loop(1, n_groups, merged, 0)

    def post(r):
        rows = pl.ds(pl.multiple_of(r * PREP_ROWS, PREP_ROWS), PREP_ROWS)
        o = o_s[rows, :]
        ms = _dot((o * o).astype(BF16), gsum) * (1.0 / HD)
        gate = gate_ref[rows, :]
        y = o * lax.rsqrt(ms + EPS) * gout_ref[...] * (gate * _sigmoid(gate))
        o_ref[rows, :] = y.astype(o_ref.dtype)

    n_tiles = seq // PREP_ROWS
    done_tiles = (n_groups - 1) * GDN_INTERLEAVE * CHUNK // PREP_ROWS
    overlapped = min(done_tiles, GDN_INTERLEAVE)
    last_first = (n_groups - 1) * GDN_INTERLEAVE

    def last_scan_with_post(i, _):
        post(i)
        drain(scan_stages(last_first + i, 1))
        return 0

    def last_scan(c, _):
        drain(scan_stages(c, 1))
        return 0

    def post_only(r, _):
        post(r)
        return 0

    lax.fori_loop(0, overlapped, last_scan_with_post, 0)
    lax.fori_loop(last_first + overlapped, last_first + GDN_INTERLEAVE, last_scan, 0)
    lax.fori_loop(overlapped, n_tiles, post_only, 0)


def _gdn(p, conv_w, alog_b, dtb_b, gout, consts, batch, seq):
    n = batch * seq
    vec = lambda w: pl.BlockSpec((1, w), lambda b: (0, 0))
    full = lambda a: pl.BlockSpec(a.shape, lambda b: (0,) * a.ndim)
    return pl.pallas_call(
        _gdn_kernel,
        grid=(batch,),
        in_specs=[
            pl.BlockSpec((seq, 3 * WG), lambda b: (b, COL_QKV_A // (3 * WG))),
            pl.BlockSpec((seq, AB_PAD), lambda b: (b, COL_AB // AB_PAD)),
            pl.BlockSpec((seq, WG), lambda b: (b, COL_GATE_A // WG)),
            full(conv_w), vec(WG), vec(WG), vec(WG),
            full(consts["gsum"]), full(consts["expand"]), full(consts["ltri"]),
        ],
        out_specs=pl.BlockSpec((seq, WG), lambda b: (b, 0)),
        out_shape=jax.ShapeDtypeStruct((n, WG), MIX_OUT),
        scratch_shapes=[pltpu.VMEM((seq, WG), F32)] * 7 + [
            pltpu.VMEM((2 * seq, WG), BF16), pltpu.VMEM((seq, WG), BF16), pltpu.VMEM((seq, WG), BF16),
            pltpu.VMEM((CHUNK, WG), F32)],
        compiler_params=pltpu.CompilerParams(vmem_limit_bytes=VMEM_LIMIT),
        name="gated_deltanet",
    )(p, p, p, conv_w, alog_b, dtb_b, gout, consts["gsum"], consts["expand"], consts["ltri"])


def _sb_kernel(qkv_ref, gout_ref, gsum_ref, sufsum_ref, o_ref, kbd_s, vbd_s, acc_s, carry_s):
    seq = qkv_ref.shape[0]
    gi = pl.program_id(1)
    blk = SB_QBLOCK

    @pl.when(gi == 0)
    def _():
        def build(j, _):
            rows = pl.ds(pl.multiple_of(j * blk, blk), blk)
            kt = qkv_ref[rows, WG:2 * WG].astype(F32).T
            row_head = lax.broadcasted_iota(jnp.int32, kt.shape, 0) // HD
            kbd_s[j] = jnp.concatenate([jnp.where(row_head == h, kt, 0.0) for h in range(HEADS)],
                                       axis=1).astype(BF16)
            vbd_s[j] = _blockdiag(qkv_ref[rows, 2 * WG:].astype(F32), HD).astype(BF16)
            return 0
        lax.fori_loop(0, seq // blk, build, 0)

    sufsum = sufsum_ref[...]
    past = lax.broadcasted_iota(jnp.int32, (blk, blk), 1) < lax.broadcasted_iota(jnp.int32, (blk, blk), 0)
    qblocks = [gi * SB_GROUP + k for k in range(SB_GROUP)]
    qs = [(qkv_ref[pl.ds(pl.multiple_of(qb * blk, blk), blk), :WG].astype(F32) * (LOG2_E * HD ** -0.5)).astype(BF16)
          for qb in qblocks]

    def step(offset, diagonal):
        js = [jnp.maximum(qb - offset, 0) for qb in qblocks]
        zs = [_dot(q, kbd_s[j]) for q, j in zip(qs, js)]
        probs = []
        for k, z in enumerate(zs):
            rows = slice(k * blk, (k + 1) * blk)
            lss, nlfs = [], []
            for h in range(HEADS):
                zh = z[:, h * blk:(h + 1) * blk]
                nlf = jnp.maximum(zh, 0.0) + jnp.log2(1.0 + jnp.exp2(-jnp.abs(zh)))
                lss.append(zh - nlf)
                nlfs.append((jnp.where(past, nlf, 0.0) if diagonal else nlf).astype(BF16))
            cs = _dot(jnp.concatenate(nlfs, axis=0), sufsum)
            per_head = []
            for h in range(HEADS):
                lanes = slice(h * blk, (h + 1) * blk)
                csh = cs[h * blk:(h + 1) * blk, :]
                carry = carry_s[rows, lanes]
                a = jnp.exp2(lss[h] + csh[:, :blk] + carry)
                if diagonal:
                    a = jnp.where(past, a, 0.0)
                carry_s[rows, lanes] = carry + csh[:, blk:]
                per_head.append(a.astype(BF16))
            probs.append(jnp.concatenate(per_head, axis=1))
        for k, (p, j) in enumerate(zip(probs, js)):
            rows = slice(k * blk, (k + 1) * blk)
            av = _dot(p, vbd_s[j])
            if not diagonal:
                av = jnp.where(qblocks[k] >= offset, av, 0.0)
            acc_s[rows, :] += av

    def largest_live_carry(offset):
        live = jnp.float32(-jnp.inf)
        for k, qb in enumerate(qblocks):
            live = jnp.maximum(live, jnp.where(qb >= offset, jnp.max(carry_s[k * blk:(k + 1) * blk, :]), -jnp.inf))
        return live

    acc_s[...] = jnp.zeros_like(acc_s)
    carry_s[...] = jnp.zeros_like(carry_s)
    step(0, True)

    def cond(state):
        offset, largest = state
        return jnp.logical_and(offset <= qblocks[-1], largest > SB_ZERO_LOG2)

    def body(state):
        offset, _ = state
        step(offset, False)
        return offset + 1, largest_live_carry(offset + 1)

    lax.while_loop(cond, body, (jnp.int32(1), largest_live_carry(1)))

    o = acc_s[...]
    ms = _dot((o * o).astype(BF16), gsum_ref[...]) * (1.0 / HD)
    o_ref[...] = (o * lax.rsqrt(ms + EPS) * gout_ref[...]).astype(o_ref.dtype)


def _stick_breaking(p, gout, consts, batch, seq):
    n = batch * seq
    nq = seq // SB_QBLOCK
    ng = nq // SB_GROUP
    rows = SB_GROUP * SB_QBLOCK
    full = lambda a: pl.BlockSpec(a.shape, lambda b, i: (0,) * a.ndim)
    return pl.pallas_call(
        _sb_kernel,
        grid=(batch, ng),
        in_specs=[
            pl.BlockSpec((seq, 3 * WG), lambda b, i: (b, 0)),
            pl.BlockSpec((1, WG), lambda b, i: (0, 0)),
            full(consts["gsum"]), full(consts["sufsum"]),
        ],
        out_specs=pl.BlockSpec((rows, WG), lambda b, i: (b * ng + i, 0)),
        out_shape=jax.ShapeDtypeStruct((n, WG), MIX_OUT),
        scratch_shapes=[
            pltpu.VMEM((nq, WG, HEADS * SB_QBLOCK), BF16),
            pltpu.VMEM((nq, HEADS * SB_QBLOCK, WG), BF16),
            pltpu.VMEM((rows, WG), F32),
            pltpu.VMEM((rows, HEADS * SB_QBLOCK), F32),
        ],
        compiler_params=pltpu.CompilerParams(
            dimension_semantics=("arbitrary", "arbitrary"), vmem_limit_bytes=VMEM_LIMIT),
        name="stick_breaking",
    )(p, gout, consts["gsum"], consts["sufsum"])


def _outmlp_kernel(x_ref, ya_ref, yb_ref, yc_ref, yd_ref, mod_ref, g2_ref, gfin_ref,
                   wout_ref, w1_ref, w2_ref, o_ref, *, ff_chunk, final):
    m = mod_ref[0]
    sub = x_ref.shape[0] // TOKEN_SUBTILES
    parts = [slice(k * sub, (k + 1) * sub) for k in range(TOKEN_SUBTILES)]
    ys = [jnp.concatenate([ref[r, :].astype(BF16) for ref in (ya_ref, yb_ref, yc_ref, yd_ref)], axis=1)
          for r in parts]
    xs = [x_ref[r, :] + m[2:3] * _dot(y, wout_ref[...]) for r, y in zip(parts, ys)]
    hs = [(_rms(x, g2_ref[...]) * (1.0 + m[4:5]) + m[3:4]).astype(BF16) for x in xs]
    d_ff = w1_ref.shape[1]
    ffs = [jnp.zeros(x.shape, F32) for x in xs]
    for k in range(0, d_ff, ff_chunk):
        acts = [jnp.maximum(_dot(h, w1_ref[:, k:k + ff_chunk]), 0.0) for h in hs]
        ffs = [ff + _dot((a * a).astype(BF16), w2_ref[k:k + ff_chunk, :]) for ff, a in zip(ffs, acts)]
    for r, x, ff in zip(parts, xs, ffs):
        x = x + m[5:6] * ff
        if final:
            x = _rms(x, gfin_ref[...])
        o_ref[r, :] = x


def _outmlp(x2, ys, mod_l, g2, gfin, w_out, w_ff1, w_ff2, layer, seq, tm, final):
    n, d = x2.shape
    d_ff = w_ff1.shape[-1]
    tiles_per_seq = seq // tm
    tile = lambda w: pl.BlockSpec((tm, w), lambda i: (i, 0))
    const = lambda a: pl.BlockSpec((None,) + a.shape[1:], lambda i: (layer, 0, 0), pipeline_mode=pl.Buffered(1))
    return pl.pallas_call(
        functools.partial(_outmlp_kernel, ff_chunk=min(d_ff, 1024), final=final),
        grid=(n // tm,),
        in_specs=[
            tile(d), tile(WG), tile(WG), tile(WG), tile(WG),
            pl.BlockSpec((1, N_MOD, d), lambda i: (i // tiles_per_seq, 0, 0)),
            pl.BlockSpec((1, d), lambda i: (0, 0)),
            pl.BlockSpec((1, d), lambda i: (0, 0)),
            const(w_out), const(w_ff1), const(w_ff2),
        ],
        out_specs=tile(d),
        out_shape=jax.ShapeDtypeStruct((n, d), F32),
        compiler_params=pltpu.CompilerParams(vmem_limit_bytes=VMEM_LIMIT),
        name="outproj_mlp",
    )(x2, *ys, mod_l, g2, gfin, w_out, w_ff1, w_ff2)


def _constants():
    idx = jnp.arange(WG)
    gsum = (idx[:, None] // HD == idx[None, :] // HD).astype(BF16)
    ab = jnp.arange(AB_PAD)[:, None]
    lane = jnp.arange(2 * WG)[None, :]
    expand = (ab == (lane // WG) * HEADS + (lane % WG) // HD).astype(BF16)
    r = jnp.arange(PREP_ROWS)
    ltri = ((r[:, None] // CHUNK == r[None, :] // CHUNK) & (r[:, None] >= r[None, :])).astype(BF16)
    k = jnp.arange(SB_QBLOCK)
    suf = (k[:, None] > k[None, :]).astype(BF16)
    sufsum = -jnp.concatenate([suf, jnp.ones((SB_QBLOCK, SB_QBLOCK), BF16)], axis=1)
    return {"gsum": gsum, "expand": expand, "ltri": ltri, "sufsum": sufsum}


def _pad_w_in(w_in):
    o = 0
    parts = {}
    for name, size in (("qkv_a", 3 * WG), ("ab", 2 * HEADS), ("gate_a", WG), ("u_b", WG), ("v_b", WG),
                       ("x_c", WG), ("qkv_d", 3 * WG)):
        parts[name] = w_in[..., o:o + size]
        o += size
    ab = jnp.pad(parts["ab"], ((0, 0), (0, 0), (0, AB_PAD - 2 * HEADS)))
    return jnp.concatenate([parts["qkv_a"], parts["gate_a"], ab, parts["qkv_d"], parts["u_b"], parts["v_b"],
                            parts["x_c"]], axis=-1).astype(BF16)


def _per_head_lanes(v):
    return jnp.repeat(v, HD, axis=-1)[..., None, :]


def kernel(x, c, w_mod, b_mod, g_norm1, g_norm2, w_in, conv_a, a_log, dt_bias, g_out_a, g_ln_b, w_sg, b_sg,
           w_pool, s_pool, g_out_d, w_out, w_ff1, w_ff2, g_final):
    batch, seq, d = x.shape
    depth = w_mod.shape[0]
    assert seq % (SB_GROUP * SB_QBLOCK) == 0 and seq % (GDN_INTERLEAVE * CHUNK) == 0 and d == HEADS * WG
    tm = min(512, seq)
    consts = _constants()

    mod = _modulation(c, w_mod, b_mod).reshape(depth, batch, N_MOD, d)
    w_in_p = _pad_w_in(w_in)
    w_out_b, w_ff1_b, w_ff2_b = w_out.astype(BF16), w_ff1.astype(BF16), w_ff2.astype(BF16)
    alog_b, dtb_b = _per_head_lanes(a_log), _per_head_lanes(dt_bias)
    wsg_cat = jnp.transpose(w_sg, (0, 2, 1, 3)).reshape(depth, SG_BLOCK, HEADS * SG_BLOCK)
    bsg_packed = jnp.repeat(jnp.transpose(b_sg, (0, 2, 1)), HD, axis=-1)
    wpool_bd = (jnp.eye(HEADS, dtype=F32)[None, :, None, :, None] * w_pool[:, :, :, None, :]).reshape(
        depth, WG, WG).astype(BF16)
    row = lambda v: v[:, None, :]

    x2 = x.reshape(batch * seq, d)
    for l in range(depth):
        sg = (row(g_ln_b)[l], wsg_cat[l], bsg_packed[l], wpool_bd[l], row(s_pool)[l])
        p, qkv_d, y_b, y_c = _inproj(x2, mod[l], row(g_norm1)[l], w_in_p, l, sg, consts, seq, tm)
        y_a = _gdn(p, conv_a[l], alog_b[l], dtb_b[l], row(g_out_a)[l], consts, batch, seq)
        y_d = _stick_breaking(qkv_d, row(g_out_d)[l], consts, batch, seq)
        x2 = _outmlp(x2, (y_a, y_b, y_c, y_d), mod[l], row(g_norm2)[l], g_final[None, :], w_out_b,
                     w_ff1_b, w_ff2_b, l, seq, tm, final=(l == depth - 1))
    return x2.reshape(batch, seq, d)
```

```python
import functools

import jax
import jax.numpy as jnp
from jax import lax
from jax.experimental import pallas as pl
from jax.experimental.pallas import tpu as pltpu

F32 = jnp.float32
BF16 = jnp.bfloat16
MIX_OUT = BF16

EPS = 1e-6
N_MOD = 6
HEADS = 4
HD = 64
WG = HEADS * HD
CHUNK = 64
CONV_W = 4
CONV_HALO = 8
SG_BLOCK = 128
SB_QBLOCK = 128
POOL_WINDOWS = (2, 4, 8, 16)
POOL_HALO = 16
TOKEN_SUBTILES = 2
PREP_ROWS = 256
GDN_INTERLEAVE = 8
SB_GROUP = 8
LOG2_E = 1.4426950408889634
SB_ZERO_LOG2 = -160.0

COL_QKV_A, COL_GATE_A, COL_AB, COL_QKV_D, COL_U_B, COL_V_B, COL_X_C = 0, 768, 1024, 1152, 1920, 2176, 2432
AB_PAD = 128
P_GDN = COL_QKV_D
P_SB = COL_U_B
P_PAD = COL_X_C + 256

VMEM_LIMIT = 56 * 1024 * 1024


def _dot(a, b):
    return jnp.dot(a, b, preferred_element_type=F32)


def _dot_nt(a, b):
    return lax.dot_general(a, b, (((1,), (1,)), ((), ())), preferred_element_type=F32)


def _split(x):
    hi = x.astype(BF16)
    lo = (x - hi.astype(F32)).astype(BF16)
    return hi, lo


def _dot_exact_rhs(x, m):
    hi, lo = _split(x)
    return _dot(hi, m) + _dot(lo, m)


def _dot_exact_lhs(m, x):
    hi, lo = _split(x)
    return _dot(m, hi) + _dot(m, lo)


def _sigmoid(x):
    return 1.0 / (1.0 + jnp.exp(-x))


def _softplus(x):
    return jnp.maximum(x, 0.0) + jnp.log(1.0 + jnp.exp(-jnp.abs(x)))


def _lane_head(shape):
    return lax.broadcasted_iota(jnp.int32, shape, 1) // HD


def _blockdiag(y, width):
    lane_blk = lax.broadcasted_iota(jnp.int32, y.shape, 1) // width
    return jnp.concatenate([jnp.where(lane_blk == h, y, 0.0) for h in range(HEADS)], axis=0)


def _shift_rows(x, k):
    return pltpu.roll(x, k, axis=0)


def _mod_kernel(c_ref, w_ref, b_ref, o_ref):
    c = c_ref[...]
    cond = c * _sigmoid(c)
    o_ref[0] = _dot(cond.astype(BF16), w_ref[0].astype(BF16)) + b_ref[0]


def _modulation(c, w_mod, b_mod):
    depth, d, n6 = w_mod.shape
    b = c.shape[0]
    nblk = n6 // d
    return pl.pallas_call(
        _mod_kernel,
        grid=(depth, nblk),
        in_specs=[
            pl.BlockSpec((b, d), lambda l, j: (0, 0)),
            pl.BlockSpec((1, d, d), lambda l, j: (l, 0, j)),
            pl.BlockSpec((1, 1, d), lambda l, j: (l, 0, j)),
        ],
        out_specs=pl.BlockSpec((1, b, d), lambda l, j: (l, 0, j)),
        out_shape=jax.ShapeDtypeStruct((depth, b, n6), F32),
        name="modulation",
    )(c, w_mod, b_mod.reshape(depth, 1, n6))


def _rms(x, g):
    return x * lax.rsqrt(jnp.mean(x * x, axis=-1, keepdims=True) + EPS) * g


def _sg_pool_stages(rows, u, v, xc, halo, t0, gln_ref, wsg_ref, bsg_ref, wpool_ref, spool_ref, gsum, yb_ref,
                    yc_ref):
    blocks = [slice(k * SG_BLOCK, (k + 1) * SG_BLOCK) for k in range(rows // SG_BLOCK)]
    t_chunk = lax.broadcasted_iota(jnp.int32, wsg_ref.shape, 0) // CHUNK
    s_chunk = (lax.broadcasted_iota(jnp.int32, wsg_ref.shape, 1) % SG_BLOCK) // CHUNK
    wm = jnp.where(t_chunk >= s_chunk, wsg_ref[...], 0.0).astype(BF16)
    whole = slice(0, rows)
    mean = _dot(v(whole).astype(BF16), gsum) * (1.0 / HD)
    yield
    vc = v(whole) - mean
    var = _dot((vc * vc).astype(BF16), gsum) * (1.0 / HD)
    lane_grp = _lane_head((PREP_ROWS, WG))
    win = jnp.zeros((PREP_ROWS, WG), jnp.int32)
    for g, w in enumerate(POOL_WINDOWS):
        win = jnp.where(lane_grp == g, w, win)
    pooled = []
    for piece in range(rows // PREP_ROWS):
        cur = xc(slice(piece * PREP_ROWS, (piece + 1) * PREP_ROWS))
        prev = halo if piece == 0 else xc(slice(piece * PREP_ROWS - POOL_HALO, piece * PREP_ROWS))
        s = jnp.concatenate([prev, cur], axis=0)
        total = jnp.zeros((PREP_ROWS, WG), F32)
        k = 1
        for g, w in enumerate(POOL_WINDOWS):
            while k < w:
                s = s + _shift_rows(s, k)
                k *= 2
            total = jnp.where(lane_grp == g, s[POOL_HALO:, :], total)
        t = t0 + piece * PREP_ROWS + lax.broadcasted_iota(jnp.int32, (PREP_ROWS, WG), 0)
        count = jnp.minimum(t + 1, win).astype(F32)
        pooled.append((total / count - cur).astype(BF16))
    yield
    vn = vc * lax.rsqrt(var + EPS) * gln_ref[...]
    vns = [_blockdiag(vn[r, :], HD).astype(BF16) for r in blocks]
    for r, vn in zip(blocks, vns):
        yb_ref[r, :] = (u(r) * (_dot(wm, vn) + bsg_ref[...])).astype(yb_ref.dtype)
    for piece, pl_ in enumerate(pooled):
        yc_ref[piece * PREP_ROWS:(piece + 1) * PREP_ROWS, :] = (
            _dot(pl_, wpool_ref[...]) * spool_ref[...]).astype(yc_ref.dtype)


def _inproj_kernel(x_ref, mod_ref, g_ref, w_ref, convw_ref, gln_ref, wsg_ref, bsg_ref, wpool_ref, spool_ref,
                   gsum_ref, o_ref, osb_ref, yb_ref, yc_ref, uvx_s, halo_s, chalo_s, *, tiles_per_seq, n_tiles):
    i = pl.program_id(0)
    tm = x_ref.shape[0]

    @pl.when(i == 0)
    def _():
        uvx_s[...] = jnp.zeros_like(uvx_s)
        halo_s[...] = jnp.zeros_like(halo_s)
        chalo_s[...] = jnp.zeros_like(chalo_s)

    prev_tile = jnp.maximum(i - 1, 0)
    prev_buf = (i + 1) % 2
    u, v, xc = (functools.partial(lambda k, r: uvx_s[prev_buf, r, k * WG:(k + 1) * WG], k) for k in range(3))
    halo = jnp.where(prev_tile % tiles_per_seq == 0, 0.0, halo_s[...])
    t0 = (prev_tile % tiles_per_seq) * tm
    side = _sg_pool_stages(tm, u, v, xc, halo, t0, gln_ref, wsg_ref, bsg_ref, wpool_ref, spool_ref, gsum_ref[...],
                           yb_ref, yc_ref)

    m = mod_ref[0]
    sub = tm // TOKEN_SUBTILES
    parts = [slice(k * sub, (k + 1) * sub) for k in range(TOKEN_SUBTILES)]
    hs = [(_rms(x_ref[r, :], g_ref[...]) * (1.0 + m[1:2]) + m[0:1]).astype(BF16) for r in parts]
    next(side)
    new_halo = xc(slice(tm - POOL_HALO, tm))
    tile = jnp.minimum(i, n_tiles - 1)
    conv_prev = jnp.where(tile % tiles_per_seq == 0, 0.0, chalo_s[(tile + 1) % 2])
    for r, h in zip(parts, hs):
        p = _dot(h, w_ref[...])
        qa = p[:, :3 * WG]
        ext = jnp.concatenate([conv_prev, qa], axis=0)
        acc = qa * convw_ref[CONV_W - 1:CONV_W, :]
        for k in range(1, CONV_W):
            acc = acc + _shift_rows(ext, k)[CONV_HALO:, :] * convw_ref[CONV_W - 1 - k:CONV_W - k, :]
        o_ref[r, :3 * WG] = acc * _sigmoid(acc)
        o_ref[r, 3 * WG:] = p[:, 3 * WG:P_GDN]
        osb_ref[r, :] = p[:, P_GDN:P_SB].astype(osb_ref.dtype)
        uvx_s[i % 2, r, :] = p[:, P_SB:]
        conv_prev = qa[sub - CONV_HALO:, :]
        next(side, None)
    for _ in side:
        pass
    halo_s[...] = new_halo
    chalo_s[tile % 2] = conv_prev


def _inproj(x2, mod_l, g1, w_in_p, layer, conv_w, sg, consts, seq, tm):
    n, d = x2.shape
    tiles_per_seq = seq // tm
    n_tiles = n // tm
    cur = lambda i: jnp.minimum(i, n_tiles - 1)
    full = lambda a: pl.BlockSpec(a.shape, lambda i: (0,) * a.ndim)
    mix_out = pl.BlockSpec((tm, WG), lambda i: (jnp.maximum(i - 1, 0), 0))
    return pl.pallas_call(
        functools.partial(_inproj_kernel, tiles_per_seq=tiles_per_seq, n_tiles=n_tiles),
        grid=(n_tiles + 1,),
        in_specs=[
            pl.BlockSpec((tm, d), lambda i: (cur(i), 0)),
            pl.BlockSpec((1, N_MOD, d), lambda i: (cur(i) // tiles_per_seq, 0, 0)),
            pl.BlockSpec((1, d), lambda i: (0, 0)),
            pl.BlockSpec((None, d, P_PAD), lambda i: (layer, 0, 0)),
        ] + [full(conv_w)] + [full(a) for a in sg] + [full(consts["gsum"])],
        out_specs=[pl.BlockSpec((tm, P_GDN), lambda i: (cur(i), 0)),
                   pl.BlockSpec((tm, P_SB - P_GDN), lambda i: (cur(i), 0)), mix_out, mix_out],
        out_shape=[jax.ShapeDtypeStruct((n, P_GDN), F32), jax.ShapeDtypeStruct((n, P_SB - P_GDN), BF16)]
        + [jax.ShapeDtypeStruct((n, WG), MIX_OUT)] * 2,
        scratch_shapes=[pltpu.VMEM((2, tm, 3 * WG), F32), pltpu.VMEM((POOL_HALO, WG), F32),
                        pltpu.VMEM((2, CONV_HALO, 3 * WG), F32)],
        compiler_params=pltpu.CompilerParams(dimension_semantics=("arbitrary",), vmem_limit_bytes=VMEM_LIMIT),
        name="inproj_gating_pool",
    )(x2, mod_l, g1, w_in_p, conv_w, *sg, consts["gsum"])


def _gdn_kernel(qkv_ref, ab_ref, gate_ref, alog_ref, dtb_ref, gout_ref,
                gsum_ref, expand_ref, ltri_ref, o_ref,
                q_s, k_s, v_s, beta_s, gc_s, o_s, kdu_s, lhs_s, qk_s, kd_s, state_s):
    seq = qkv_ref.shape[0]
    gsum = gsum_ref[...]

    def prep_pair(i, _):
        tiles = [pl.ds(pl.multiple_of((2 * i + t) * PREP_ROWS, PREP_ROWS), PREP_ROWS) for t in range(2)]
        ys = [qkv_ref[rows, :] for rows in tiles]
        qs = [y[:, :WG] for y in ys]
        ks = [y[:, WG:2 * WG] for y in ys]
        qss = [_dot((q * q).astype(BF16), gsum) for q in qs]
        kss = [_dot((kk * kk).astype(BF16), gsum) for kk in ks]
        abxs = [_dot_exact_rhs(ab_ref[rows, :], expand_ref[...]) for rows in tiles]
        gs = [-jnp.exp(alog_ref[...]) * _softplus(abx[:, :WG] + dtb_ref[...]) for abx in abxs]
        gcs = [_dot_exact_lhs(ltri_ref[...], g) for g in gs]
        for rows, y, q, kk, q2, k2, abx, gc in zip(tiles, ys, qs, ks, qss, kss, abxs, gcs):
            q_s[rows, :] = q * lax.rsqrt(q2 + EPS) * (HD ** -0.5)
            k_s[rows, :] = kk * lax.rsqrt(k2 + EPS)
            v_s[rows, :] = y[:, 2 * WG:]
            beta_s[rows, :] = _sigmoid(abx[:, WG:])
            gc_s[rows, :] = gc
        return 0

    lax.fori_loop(0, seq // (2 * PREP_ROWS), prep_pair, 0)

    state_s[...] = jnp.zeros_like(state_s)
    shape = (CHUNK, WG)
    row = lax.broadcasted_iota(jnp.int32, shape, 0)
    col = lax.broadcasted_iota(jnp.int32, shape, 1) % CHUNK
    eye = row == col
    incl = row >= col
    strict = row > col

    def bd(y):
        return _blockdiag(y, HD).astype(BF16)

    def intra_stages(i):
        chunks = [i * GDN_INTERLEAVE + k for k in range(GDN_INTERLEAVE)]
        rows = [pl.ds(pl.multiple_of(c * CHUNK, CHUNK), CHUNK) for c in chunks]
        lows, ts = [], []
        for c, r in zip(chunks, rows):
            qn, kn, gc = q_s[r, :], k_s[r, :], gc_s[r, :]
            kb = kn * beta_s[r, :]
            a = _dot_nt(jnp.concatenate([kb, qn], axis=0).astype(BF16), bd(kn))
            yield
            gc_row = jnp.sum(jnp.where(eye, gc, 0.0), axis=0, keepdims=True)
            decay = jnp.where(incl, jnp.exp(gc - gc_row), 0.0)
            low = jnp.where(strict, a[:CHUNK] * decay, 0.0)
            qk_s[r, :] = (a[CHUNK:] * decay).astype(BF16)
            kd_t = (kn * jnp.exp(gc[CHUNK - 1:CHUNK, :] - gc)).T
            kd_s[r, :] = jnp.concatenate([kd_t[h * HD:(h + 1) * HD, :] for h in range(HEADS)],
                                         axis=1).astype(BF16)
            lows.append(low)
            ts.append(jnp.where(eye, 1.0, 0.0) - jnp.where(row // 2 == col // 2, low, 0.0))
        s = 2
        while s < CHUNK:
            join = jnp.logical_and(row // (2 * s) == col // (2 * s), row // s != col // s)
            tes = []
            for t, low in zip(ts, lows):
                tes.append(_dot(t.astype(BF16), bd(jnp.where(join, low, 0.0))))
                yield
            for k, te in enumerate(tes):
                ts[k] = ts[k] - _dot(te.astype(BF16), bd(ts[k]))
                yield
            s *= 2
        uws = []
        for r, t in zip(rows, ts):
            beta, gc = beta_s[r, :], gc_s[r, :]
            rhs = jnp.concatenate([bd(v_s[r, :] * beta), bd(k_s[r, :] * beta * jnp.exp(gc))], axis=1)
            uws.append(_dot(t.astype(BF16), rhs))
            yield
        for c, r, uw in zip(chunks, rows, uws):
            u, w = uw[:, :WG], uw[:, WG:]
            tr = _dot(jnp.concatenate([kd_s[r, :], qk_s[r, :]], axis=0), jnp.concatenate([bd(w), bd(u)], axis=1))
            yield
            q_eff = q_s[r, :] * jnp.exp(gc_s[r, :]) - tr[CHUNK:, :WG]
            lhs_s[pl.ds(pl.multiple_of(c * 2 * CHUNK, 2 * CHUNK), CHUNK), :] = q_eff.astype(BF16)
            lhs_s[pl.ds(pl.multiple_of(c * 2 * CHUNK + CHUNK, CHUNK), CHUNK), :] = tr[:CHUNK, :WG].astype(BF16)
            kdu_s[r, :] = tr[:CHUNK, WG:]
            o_s[r, :] = tr[CHUNK:, WG:]

    def scan_stages(first, count):
        state = state_s[...]
        for k in range(count):
            c = first + k
            rows = pl.ds(pl.multiple_of(c * CHUNK, CHUNK), CHUNK)
            ws = _dot(lhs_s[pl.ds(pl.multiple_of(c * 2 * CHUNK, 2 * CHUNK), 2 * CHUNK), :], bd(state))
            yield
            gl = gc_s[pl.ds(pl.multiple_of(c * CHUNK + CHUNK - 8, 8), 8), :][7:8, :]
            state = state * jnp.exp(gl) - ws[CHUNK:] + kdu_s[rows, :]
            o_s[rows, :] = o_s[rows, :] + ws[:CHUNK]
        state_s[...] = state

    def drain(stages):
        for _ in stages:
            pass

    def interleave(main, side, main_per_side):
        main_live = side_live = True
        while main_live or side_live:
            if side_live:
                side_live = next(side, 0) is None
            for _ in range(main_per_side):
                if main_live:
                    main_live = next(main, 0) is None

    n_groups = seq // (CHUNK * GDN_INTERLEAVE)
    intra_matmuls = GDN_INTERLEAVE * (3 + 2 * (CHUNK.bit_length() - 2))
    scan_matmuls = GDN_INTERLEAVE

    drain(intra_stages(0))

    def merged(g, _):
        interleave(intra_stages(g), scan_stages((g - 1) * GDN_INTERLEAVE, GDN_INTERLEAVE),
                   intra_matmuls // scan_matmuls)
        return 0

    lax.fori_loop(1, n_groups, merged, 0)

    def post(r):
        rows = pl.ds(pl.multiple_of(r * PREP_ROWS, PREP_ROWS), PREP_ROWS)
        o = o_s[rows, :]
        ms = _dot((o * o).astype(BF16), gsum) * (1.0 / HD)
        gate = gate_ref[rows, :]
        y = o * lax.rsqrt(ms + EPS) * gout_ref[...] * (gate * _sigmoid(gate))
        o_ref[rows, :] = y.astype(o_ref.dtype)

    n_tiles = seq // PREP_ROWS
    done_tiles = (n_groups - 1) * GDN_INTERLEAVE * CHUNK // PREP_ROWS
    overlapped = min(done_tiles, GDN_INTERLEAVE)
    last_first = (n_groups - 1) * GDN_INTERLEAVE

    def last_scan_with_post(i, _):
        post(i)
        drain(scan_stages(last_first + i, 1))
        return 0

    def last_scan(c, _):
        drain(scan_stages(c, 1))
        return 0

    def post_only(r, _):
        post(r)
        return 0

    lax.fori_loop(0, overlapped, last_scan_with_post, 0)
    lax.fori_loop(last_first + overlapped, last_first + GDN_INTERLEAVE, last_scan, 0)
    lax.fori_loop(overlapped, n_tiles, post_only, 0)


def _gdn(p, alog_b, dtb_b, gout, consts, batch, seq):
    n = batch * seq
    vec = lambda w: pl.BlockSpec((1, w), lambda b: (0, 0))
    full = lambda a: pl.BlockSpec(a.shape, lambda b: (0,) * a.ndim)
    return pl.pallas_call(
        _gdn_kernel,
        grid=(batch,),
        in_specs=[
            pl.BlockSpec((seq, 3 * WG), lambda b: (b, COL_QKV_A // (3 * WG))),
            pl.BlockSpec((seq, AB_PAD), lambda b: (b, COL_AB // AB_PAD)),
            pl.BlockSpec((seq, WG), lambda b: (b, COL_GATE_A // WG)),
            vec(WG), vec(WG), vec(WG),
            full(consts["gsum"]), full(consts["expand"]), full(consts["ltri"]),
        ],
        out_specs=pl.BlockSpec((seq, WG), lambda b: (b, 0)),
        out_shape=jax.ShapeDtypeStruct((n, WG), MIX_OUT),
        scratch_shapes=[pltpu.VMEM((seq, WG), F32)] * 7 + [
            pltpu.VMEM((2 * seq, WG), BF16), pltpu.VMEM((seq, WG), BF16), pltpu.VMEM((seq, WG), BF16),
            pltpu.VMEM((CHUNK, WG), F32)],
        compiler_params=pltpu.CompilerParams(vmem_limit_bytes=VMEM_LIMIT),
        name="gated_deltanet",
    )(p, p, p, alog_b, dtb_b, gout, consts["gsum"], consts["expand"], consts["ltri"])


def _sb_kernel(qkv_ref, gout_ref, gsum_ref, sufsum_ref, o_ref, kbd_s, vbd_s, acc_s, carry_s):
    seq = qkv_ref.shape[0]
    gi = pl.program_id(1)
    blk = SB_QBLOCK

    @pl.when(gi == 0)
    def _():
        def build(j, _):
            rows = pl.ds(pl.multiple_of(j * blk, blk), blk)
            kt = qkv_ref[rows, WG:2 * WG].astype(F32).T
            row_head = lax.broadcasted_iota(jnp.int32, kt.shape, 0) // HD
            kbd_s[j] = jnp.concatenate([jnp.where(row_head == h, kt, 0.0) for h in range(HEADS)],
                                       axis=1).astype(BF16)
            vbd_s[j] = _blockdiag(qkv_ref[rows, 2 * WG:].astype(F32), HD).astype(BF16)
            return 0
        lax.fori_loop(0, seq // blk, build, 0)

    sufsum = sufsum_ref[...]
    past = lax.broadcasted_iota(jnp.int32, (blk, blk), 1) < lax.broadcasted_iota(jnp.int32, (blk, blk), 0)
    qblocks = [gi * SB_GROUP + k for k in range(SB_GROUP)]
    qs = [(qkv_ref[pl.ds(pl.multiple_of(qb * blk, blk), blk), :WG].astype(F32) * (LOG2_E * HD ** -0.5)).astype(BF16)
          for qb in qblocks]

    def step(offset, diagonal):
        js = [jnp.maximum(qb - offset, 0) for qb in qblocks]
        zs = [_dot(q, kbd_s[j]) for q, j in zip(qs, js)]
        probs = []
        for k, z in enumerate(zs):
            rows = slice(k * blk, (k + 1) * blk)
            lss, nlfs = [], []
            for h in range(HEADS):
                zh = z[:, h * blk:(h + 1) * blk]
                nlf = jnp.maximum(zh, 0.0) + jnp.log2(1.0 + jnp.exp2(-jnp.abs(zh)))
                lss.append(zh - nlf)
                nlfs.append((jnp.where(past, nlf, 0.0) if diagonal else nlf).astype(BF16))
            cs = _dot(jnp.concatenate(nlfs, axis=0), sufsum)
            per_head = []
            for h in range(HEADS):
                lanes = slice(h * blk, (h + 1) * blk)
                csh = cs[h * blk:(h + 1) * blk, :]
                carry = carry_s[rows, lanes]
                a = jnp.exp2(lss[h] + csh[:, :blk] + carry)
                if diagonal:
                    a = jnp.where(past, a, 0.0)
                carry_s[rows, lanes] = carry + csh[:, blk:]
                per_head.append(a.astype(BF16))
            probs.append(jnp.concatenate(per_head, axis=1))
        for k, (p, j) in enumerate(zip(probs, js)):
            rows = slice(k * blk, (k + 1) * blk)
            av = _dot(p, vbd_s[j])
            if not diagonal:
                av = jnp.where(qblocks[k] >= offset, av, 0.0)
            acc_s[rows, :] += av

    def largest_live_carry(offset):
        live = jnp.float32(-jnp.inf)
        for k, qb in enumerate(qblocks):
            live = jnp.maximum(live, jnp.where(qb >= offset, jnp.max(carry_s[k * blk:(k + 1) * blk, :]), -jnp.inf))
        return live

    acc_s[...] = jnp.zeros_like(acc_s)
    carry_s[...] = jnp.zeros_like(carry_s)
    step(0, True)

    def cond(state):
        offset, largest = state
        return jnp.logical_and(offset <= qblocks[-1], largest > SB_ZERO_LOG2)

    def body(state):
        offset, _ = state
        step(offset, False)
        return offset + 1, largest_live_carry(offset + 1)

    lax.while_loop(cond, body, (jnp.int32(1), largest_live_carry(1)))

    o = acc_s[...]
    ms = _dot((o * o).astype(BF16), gsum_ref[...]) * (1.0 / HD)
    o_ref[...] = (o * lax.rsqrt(ms + EPS) * gout_ref[...]).astype(o_ref.dtype)


def _stick_breaking(p, gout, consts, batch, seq):
    n = batch * seq
    nq = seq // SB_QBLOCK
    ng = nq // SB_GROUP
    rows = SB_GROUP * SB_QBLOCK
    full = lambda a: pl.BlockSpec(a.shape, lambda b, i: (0,) * a.ndim)
    return pl.pallas_call(
        _sb_kernel,
        grid=(batch, ng),
        in_specs=[
            pl.BlockSpec((seq, 3 * WG), lambda b, i: (b, 0)),
            pl.BlockSpec((1, WG), lambda b, i: (0, 0)),
            full(consts["gsum"]), full(consts["sufsum"]),
        ],
        out_specs=pl.BlockSpec((rows, WG), lambda b, i: (b * ng + i, 0)),
        out_shape=jax.ShapeDtypeStruct((n, WG), MIX_OUT),
        scratch_shapes=[
            pltpu.VMEM((nq, WG, HEADS * SB_QBLOCK), BF16),
            pltpu.VMEM((nq, HEADS * SB_QBLOCK, WG), BF16),
            pltpu.VMEM((rows, WG), F32),
            pltpu.VMEM((rows, HEADS * SB_QBLOCK), F32),
        ],
        compiler_params=pltpu.CompilerParams(
            dimension_semantics=("arbitrary", "arbitrary"), vmem_limit_bytes=VMEM_LIMIT),
        name="stick_breaking",
    )(p, gout, consts["gsum"], consts["sufsum"])


def _outmlp_kernel(x_ref, ya_ref, yb_ref, yc_ref, yd_ref, mod_ref, g2_ref, gfin_ref,
                   wout_ref, w1_ref, w2_ref, o_ref, *, ff_chunk, final):
    m = mod_ref[0]
    sub = x_ref.shape[0] // TOKEN_SUBTILES
    parts = [slice(k * sub, (k + 1) * sub) for k in range(TOKEN_SUBTILES)]
    ys = [jnp.concatenate([ref[r, :].astype(BF16) for ref in (ya_ref, yb_ref, yc_ref, yd_ref)], axis=1)
          for r in parts]
    xs = [x_ref[r, :] + m[2:3] * _dot(y, wout_ref[...]) for r, y in zip(parts, ys)]
    hs = [(_rms(x, g2_ref[...]) * (1.0 + m[4:5]) + m[3:4]).astype(BF16) for x in xs]
    d_ff = w1_ref.shape[1]
    ffs = [jnp.zeros(x.shape, F32) for x in xs]
    for k in range(0, d_ff, ff_chunk):
        acts = [jnp.maximum(_dot(h, w1_ref[:, k:k + ff_chunk]), 0.0) for h in hs]
        ffs = [ff + _dot((a * a).astype(BF16), w2_ref[k:k + ff_chunk, :]) for ff, a in zip(ffs, acts)]
    for r, x, ff in zip(parts, xs, ffs):
        x = x + m[5:6] * ff
        if final:
            x = _rms(x, gfin_ref[...])
        o_ref[r, :] = x


def _outmlp(x2, ys, mod_l, g2, gfin, w_out, w_ff1, w_ff2, layer, seq, tm, final):
    n, d = x2.shape
    d_ff = w_ff1.shape[-1]
    tiles_per_seq = seq // tm
    tile = lambda w: pl.BlockSpec((tm, w), lambda i: (i, 0))
    const = lambda a: pl.BlockSpec((None,) + a.shape[1:], lambda i: (layer, 0, 0), pipeline_mode=pl.Buffered(1))
    return pl.pallas_call(
        functools.partial(_outmlp_kernel, ff_chunk=min(d_ff, 1024), final=final),
        grid=(n // tm,),
        in_specs=[
            tile(d), tile(WG), tile(WG), tile(WG), tile(WG),
            pl.BlockSpec((1, N_MOD, d), lambda i: (i // tiles_per_seq, 0, 0)),
            pl.BlockSpec((1, d), lambda i: (0, 0)),
            pl.BlockSpec((1, d), lambda i: (0, 0)),
            const(w_out), const(w_ff1), const(w_ff2),
        ],
        out_specs=tile(d),
        out_shape=jax.ShapeDtypeStruct((n, d), F32),
        compiler_params=pltpu.CompilerParams(vmem_limit_bytes=VMEM_LIMIT),
        name="outproj_mlp",
    )(x2, *ys, mod_l, g2, gfin, w_out, w_ff1, w_ff2)


def _constants():
    idx = jnp.arange(WG)
    gsum = (idx[:, None] // HD == idx[None, :] // HD).astype(BF16)
    ab = jnp.arange(AB_PAD)[:, None]
    lane = jnp.arange(2 * WG)[None, :]
    expand = (ab == (lane // WG) * HEADS + (lane % WG) // HD).astype(BF16)
    r = jnp.arange(PREP_ROWS)
    ltri = ((r[:, None] // CHUNK == r[None, :] // CHUNK) & (r[:, None] >= r[None, :])).astype(BF16)
    k = jnp.arange(SB_QBLOCK)
    suf = (k[:, None] > k[None, :]).astype(BF16)
    sufsum = -jnp.concatenate([suf, jnp.ones((SB_QBLOCK, SB_QBLOCK), BF16)], axis=1)
    return {"gsum": gsum, "expand": expand, "ltri": ltri, "sufsum": sufsum}


def _pad_w_in(w_in):
    o = 0
    parts = {}
    for name, size in (("qkv_a", 3 * WG), ("ab", 2 * HEADS), ("gate_a", WG), ("u_b", WG), ("v_b", WG),
                       ("x_c", WG), ("qkv_d", 3 * WG)):
        parts[name] = w_in[..., o:o + size]
        o += size
    ab = jnp.pad(parts["ab"], ((0, 0), (0, 0), (0, AB_PAD - 2 * HEADS)))
    return jnp.concatenate([parts["qkv_a"], parts["gate_a"], ab, parts["qkv_d"], parts["u_b"], parts["v_b"],
                            parts["x_c"]], axis=-1).astype(BF16)


def _per_head_lanes(v):
    return jnp.repeat(v, HD, axis=-1)[..., None, :]


def kernel(x, c, w_mod, b_mod, g_norm1, g_norm2, w_in, conv_a, a_log, dt_bias, g_out_a, g_ln_b, w_sg, b_sg,
           w_pool, s_pool, g_out_d, w_out, w_ff1, w_ff2, g_final):
    batch, seq, d = x.shape
    depth = w_mod.shape[0]
    assert seq % (SB_GROUP * SB_QBLOCK) == 0 and seq % (GDN_INTERLEAVE * CHUNK) == 0 and d == HEADS * WG
    tm = min(512, seq)
    consts = _constants()

    mod = _modulation(c, w_mod, b_mod).reshape(depth, batch, N_MOD, d)
    w_in_p = _pad_w_in(w_in)
    w_out_b, w_ff1_b, w_ff2_b = w_out.astype(BF16), w_ff1.astype(BF16), w_ff2.astype(BF16)
    alog_b, dtb_b = _per_head_lanes(a_log), _per_head_lanes(dt_bias)
    wsg_cat = jnp.transpose(w_sg, (0, 2, 1, 3)).reshape(depth, SG_BLOCK, HEADS * SG_BLOCK)
    bsg_packed = jnp.repeat(jnp.transpose(b_sg, (0, 2, 1)), HD, axis=-1)
    wpool_bd = (jnp.eye(HEADS, dtype=F32)[None, :, None, :, None] * w_pool[:, :, :, None, :]).reshape(
        depth, WG, WG).astype(BF16)
    row = lambda v: v[:, None, :]

    x2 = x.reshape(batch * seq, d)
    for l in range(depth):
        sg = (row(g_ln_b)[l], wsg_cat[l], bsg_packed[l], wpool_bd[l], row(s_pool)[l])
        p, qkv_d, y_b, y_c = _inproj(x2, mod[l], row(g_norm1)[l], w_in_p, l, conv_a[l], sg, consts, seq, tm)
        y_a = _gdn(p, alog_b[l], dtb_b[l], row(g_out_a)[l], consts, batch, seq)
        y_d = _stick_breaking(qkv_d, row(g_out_d)[l], consts, batch, seq)
        x2 = _outmlp(x2, (y_a, y_b, y_c, y_d), mod[l], row(g_norm2)[l], g_final[None, :], w_out_b,
                     w_ff1_b, w_ff2_b, l, seq, tm, final=(l == depth - 1))
    return x2.reshape(batch, seq, d)
```

```python
import functools

import jax
import jax.numpy as jnp
from jax import lax
from jax.experimental import pallas as pl
from jax.experimental.pallas import tpu as pltpu

F32 = jnp.float32
BF16 = jnp.bfloat16
MIX_OUT = BF16

EPS = 1e-6
N_MOD = 6
HEADS = 4
HD = 64
WG = HEADS * HD
CHUNK = 64
CONV_W = 4
CONV_HALO = 8
SG_BLOCK = 128
SB_QBLOCK = 128
POOL_WINDOWS = (2, 4, 8, 16)
POOL_HALO = 16
TOKEN_SUBTILES = 2
PREP_ROWS = 256
GDN_INTERLEAVE = 8
SB_GROUP = 8
LOG2_E = 1.4426950408889634
SB_ZERO_LOG2 = -160.0

COL_QKV_A, COL_GATE_A, COL_AB, COL_QKV_D, COL_U_B, COL_V_B, COL_X_C = 0, 768, 1024, 1152, 1920, 2176, 2432
AB_PAD = 128
P_GDN = COL_QKV_D
P_SB = COL_U_B
P_PAD = COL_X_C + 256

VMEM_LIMIT = 56 * 1024 * 1024


def _dot(a, b):
    return jnp.dot(a, b, preferred_element_type=F32)


def _dot_nt(a, b):
    return lax.dot_general(a, b, (((1,), (1,)), ((), ())), preferred_element_type=F32)


def _split(x):
    hi = x.astype(BF16)
    lo = (x - hi.astype(F32)).astype(BF16)
    return hi, lo


def _dot_exact_rhs(x, m):
    hi, lo = _split(x)
    return _dot(hi, m) + _dot(lo, m)


def _dot_exact_lhs(m, x):
    hi, lo = _split(x)
    return _dot(m, hi) + _dot(m, lo)


def _sigmoid(x):
    return 1.0 / (1.0 + jnp.exp(-x))


def _softplus(x):
    return jnp.maximum(x, 0.0) + jnp.log(1.0 + jnp.exp(-jnp.abs(x)))


def _lane_head(shape):
    return lax.broadcasted_iota(jnp.int32, shape, 1) // HD


def _blockdiag(y, width):
    lane_blk = lax.broadcasted_iota(jnp.int32, y.shape, 1) // width
    return jnp.concatenate([jnp.where(lane_blk == h, y, 0.0) for h in range(HEADS)], axis=0)


def _shift_rows(x, k):
    return pltpu.roll(x, k, axis=0)


def _mod_kernel(c_ref, w_ref, b_ref, o_ref):
    c = c_ref[...]
    cond = c * _sigmoid(c)
    o_ref[0] = _dot(cond.astype(BF16), w_ref[0].astype(BF16)) + b_ref[0]


def _modulation(c, w_mod, b_mod):
    depth, d, n6 = w_mod.shape
    b = c.shape[0]
    nblk = n6 // d
    return pl.pallas_call(
        _mod_kernel,
        grid=(depth, nblk),
        in_specs=[
            pl.BlockSpec((b, d), lambda l, j: (0, 0)),
            pl.BlockSpec((1, d, d), lambda l, j: (l, 0, j)),
            pl.BlockSpec((1, 1, d), lambda l, j: (l, 0, j)),
        ],
        out_specs=pl.BlockSpec((1, b, d), lambda l, j: (l, 0, j)),
        out_shape=jax.ShapeDtypeStruct((depth, b, n6), F32),
        name="modulation",
    )(c, w_mod, b_mod.reshape(depth, 1, n6))


def _rms(x, g):
    return x * lax.rsqrt(jnp.mean(x * x, axis=-1, keepdims=True) + EPS) * g


def _sg_pool_stages(rows, u, v, xc, halo, t0, gln_ref, wsg_ref, bsg_ref, wpool_ref, spool_ref, gsum, yb_ref,
                    yc_ref):
    blocks = [slice(k * SG_BLOCK, (k + 1) * SG_BLOCK) for k in range(rows // SG_BLOCK)]
    t_chunk = lax.broadcasted_iota(jnp.int32, wsg_ref.shape, 0) // CHUNK
    s_chunk = (lax.broadcasted_iota(jnp.int32, wsg_ref.shape, 1) % SG_BLOCK) // CHUNK
    wm = jnp.where(t_chunk >= s_chunk, wsg_ref[...], 0.0).astype(BF16)
    whole = slice(0, rows)
    mean = _dot(v(whole).astype(BF16), gsum) * (1.0 / HD)
    yield
    vc = v(whole) - mean
    var = _dot((vc * vc).astype(BF16), gsum) * (1.0 / HD)
    lane_grp = _lane_head((PREP_ROWS, WG))
    win = jnp.zeros((PREP_ROWS, WG), jnp.int32)
    for g, w in enumerate(POOL_WINDOWS):
        win = jnp.where(lane_grp == g, w, win)
    pooled = []
    for piece in range(rows // PREP_ROWS):
        cur = xc(slice(piece * PREP_ROWS, (piece + 1) * PREP_ROWS))
        prev = halo if piece == 0 else xc(slice(piece * PREP_ROWS - POOL_HALO, piece * PREP_ROWS))
        s = jnp.concatenate([prev, cur], axis=0)
        total = jnp.zeros((PREP_ROWS, WG), F32)
        k = 1
        for g, w in enumerate(POOL_WINDOWS):
            while k < w:
                s = s + _shift_rows(s, k)
                k *= 2
            total = jnp.where(lane_grp == g, s[POOL_HALO:, :], total)
        t = t0 + piece * PREP_ROWS + lax.broadcasted_iota(jnp.int32, (PREP_ROWS, WG), 0)
        count = jnp.minimum(t + 1, win).astype(F32)
        pooled.append((total / count - cur).astype(BF16))
    yield
    vn = vc * lax.rsqrt(var + EPS) * gln_ref[...]
    vns = [_blockdiag(vn[r, :], HD).astype(BF16) for r in blocks]
    for r, vn in zip(blocks, vns):
        yb_ref[r, :] = (u(r) * (_dot(wm, vn) + bsg_ref[...])).astype(yb_ref.dtype)
    for piece, pl_ in enumerate(pooled):
        yc_ref[piece * PREP_ROWS:(piece + 1) * PREP_ROWS, :] = (
            _dot(pl_, wpool_ref[...]) * spool_ref[...]).astype(yc_ref.dtype)


def _inproj_kernel(x_ref, mod_ref, g_ref, w_ref, convw_ref, gln_ref, wsg_ref, bsg_ref, wpool_ref, spool_ref,
                   gsum_ref, o_ref, osb_ref, yb_ref, yc_ref, uvx_s, halo_s, chalo_s, *, tiles_per_seq, n_tiles):
    i = pl.program_id(0)
    tm = x_ref.shape[0]

    @pl.when(i == 0)
    def _():
        uvx_s[...] = jnp.zeros_like(uvx_s)
        halo_s[...] = jnp.zeros_like(halo_s)
        chalo_s[...] = jnp.zeros_like(chalo_s)

    prev_tile = jnp.maximum(i - 1, 0)
    prev_buf = (i + 1) % 2
    u, v, xc = (functools.partial(lambda k, r: uvx_s[prev_buf, r, k * WG:(k + 1) * WG], k) for k in range(3))
    halo = jnp.where(prev_tile % tiles_per_seq == 0, 0.0, halo_s[...])
    t0 = (prev_tile % tiles_per_seq) * tm
    side = _sg_pool_stages(tm, u, v, xc, halo, t0, gln_ref, wsg_ref, bsg_ref, wpool_ref, spool_ref, gsum_ref[...],
                           yb_ref, yc_ref)

    m = mod_ref[0]
    sub = tm // TOKEN_SUBTILES
    parts = [slice(k * sub, (k + 1) * sub) for k in range(TOKEN_SUBTILES)]
    hs = [(_rms(x_ref[r, :], g_ref[...]) * (1.0 + m[1:2]) + m[0:1]).astype(BF16) for r in parts]
    next(side)
    new_halo = xc(slice(tm - POOL_HALO, tm))
    tile = jnp.minimum(i, n_tiles - 1)
    conv_prev = jnp.where(tile % tiles_per_seq == 0, 0.0, chalo_s[(tile + 1) % 2])
    for r, h in zip(parts, hs):
        p = _dot(h, w_ref[...])
        qa = p[:, :3 * WG]
        ext = jnp.concatenate([conv_prev, qa], axis=0)
        acc = qa * convw_ref[CONV_W - 1:CONV_W, :]
        for k in range(1, CONV_W):
            acc = acc + _shift_rows(ext, k)[CONV_HALO:, :] * convw_ref[CONV_W - 1 - k:CONV_W - k, :]
        o_ref[r, :3 * WG] = acc * _sigmoid(acc)
        o_ref[r, 3 * WG:] = p[:, 3 * WG:P_GDN]
        osb_ref[r, :] = p[:, P_GDN:P_SB].astype(osb_ref.dtype)
        uvx_s[i % 2, r, :] = p[:, P_SB:]
        conv_prev = qa[sub - CONV_HALO:, :]
        next(side, None)
    for _ in side:
        pass
    halo_s[...] = new_halo
    chalo_s[tile % 2] = conv_prev


def _inproj(x2, mod_l, g1, w_in_p, layer, conv_w, sg, consts, seq, tm):
    n, d = x2.shape
    tiles_per_seq = seq // tm
    n_tiles = n // tm
    cur = lambda i: jnp.minimum(i, n_tiles - 1)
    full = lambda a: pl.BlockSpec(a.shape, lambda i: (0,) * a.ndim)
    mix_out = pl.BlockSpec((tm, WG), lambda i: (jnp.maximum(i - 1, 0), 0))
    return pl.pallas_call(
        functools.partial(_inproj_kernel, tiles_per_seq=tiles_per_seq, n_tiles=n_tiles),
        grid=(n_tiles + 1,),
        in_specs=[
            pl.BlockSpec((tm, d), lambda i: (cur(i), 0)),
            pl.BlockSpec((1, N_MOD, d), lambda i: (cur(i) // tiles_per_seq, 0, 0)),
            pl.BlockSpec((1, d), lambda i: (0, 0)),
            pl.BlockSpec((None, d, P_PAD), lambda i: (layer, 0, 0)),
        ] + [full(conv_w)] + [full(a) for a in sg] + [full(consts["gsum"])],
        out_specs=[pl.BlockSpec((tm, P_GDN), lambda i: (cur(i), 0)),
                   pl.BlockSpec((tm, P_SB - P_GDN), lambda i: (cur(i), 0)), mix_out, mix_out],
        out_shape=[jax.ShapeDtypeStruct((n, P_GDN), F32), jax.ShapeDtypeStruct((n, P_SB - P_GDN), BF16)]
        + [jax.ShapeDtypeStruct((n, WG), MIX_OUT)] * 2,
        scratch_shapes=[pltpu.VMEM((2, tm, 3 * WG), F32), pltpu.VMEM((POOL_HALO, WG), F32),
                        pltpu.VMEM((2, CONV_HALO, 3 * WG), F32)],
        compiler_params=pltpu.CompilerParams(dimension_semantics=("arbitrary",), vmem_limit_bytes=VMEM_LIMIT),
        name="inproj_gating_pool",
    )(x2, mod_l, g1, w_in_p, conv_w, *sg, consts["gsum"])


def _gdn_kernel(qkv_ref, ab_ref, gate_ref, alog_ref, dtb_ref, gout_ref,
                gsum_ref, expand_ref, ltri_ref, o_ref,
                q_s, k_s, v_s, beta_s, gc_s, o_s, kdu_s, lhs_s, qk_s, kd_s, state_s):
    seq = qkv_ref.shape[0]
    gsum = gsum_ref[...]

    def prep_stages(i):
        tiles = [slice((2 * i + t) * PREP_ROWS, (2 * i + t + 1) * PREP_ROWS) for t in range(2)]
        ys = [qkv_ref[rows, :] for rows in tiles]
        qs = [y[:, :WG] for y in ys]
        ks = [y[:, WG:2 * WG] for y in ys]
        qss = [_dot((q * q).astype(BF16), gsum) for q in qs]
        kss = [_dot((kk * kk).astype(BF16), gsum) for kk in ks]
        abxs = [_dot_exact_rhs(ab_ref[rows, :], expand_ref[...]) for rows in tiles]
        yield
        gs = [-jnp.exp(alog_ref[...]) * _softplus(abx[:, :WG] + dtb_ref[...]) for abx in abxs]
        gcs = [_dot_exact_lhs(ltri_ref[...], g) for g in gs]
        yield
        for rows, y, q, kk, q2, k2, abx, gc in zip(tiles, ys, qs, ks, qss, kss, abxs, gcs):
            q_s[rows, :] = q * lax.rsqrt(q2 + EPS) * (HD ** -0.5)
            k_s[rows, :] = kk * lax.rsqrt(k2 + EPS)
            v_s[rows, :] = y[:, 2 * WG:]
            beta_s[rows, :] = _sigmoid(abx[:, WG:])
            gc_s[rows, :] = gc

    state_s[...] = jnp.zeros_like(state_s)
    shape = (CHUNK, WG)
    row = lax.broadcasted_iota(jnp.int32, shape, 0)
    col = lax.broadcasted_iota(jnp.int32, shape, 1) % CHUNK
    eye = row == col
    incl = row >= col
    strict = row > col

    def bd(y):
        return _blockdiag(y, HD).astype(BF16)

    def intra_stages(i):
        chunks = [i * GDN_INTERLEAVE + k for k in range(GDN_INTERLEAVE)]
        rows = [pl.ds(pl.multiple_of(c * CHUNK, CHUNK), CHUNK) for c in chunks]
        lows, ts = [], []
        for c, r in zip(chunks, rows):
            qn, kn, gc = q_s[r, :], k_s[r, :], gc_s[r, :]
            kb = kn * beta_s[r, :]
            a = _dot_nt(jnp.concatenate([kb, qn], axis=0).astype(BF16), bd(kn))
            yield
            gc_row = jnp.sum(jnp.where(eye, gc, 0.0), axis=0, keepdims=True)
            decay = jnp.where(incl, jnp.exp(gc - gc_row), 0.0)
            low = jnp.where(strict, a[:CHUNK] * decay, 0.0)
            qk_s[r, :] = (a[CHUNK:] * decay).astype(BF16)
            kd_t = (kn * jnp.exp(gc[CHUNK - 1:CHUNK, :] - gc)).T
            kd_s[r, :] = jnp.concatenate([kd_t[h * HD:(h + 1) * HD, :] for h in range(HEADS)],
                                         axis=1).astype(BF16)
            lows.append(low)
            ts.append(jnp.where(eye, 1.0, 0.0) - jnp.where(row // 2 == col // 2, low, 0.0))
        s = 2
        while s < CHUNK:
            join = jnp.logical_and(row // (2 * s) == col // (2 * s), row // s != col // s)
            tes = []
            for t, low in zip(ts, lows):
                tes.append(_dot(t.astype(BF16), bd(jnp.where(join, low, 0.0))))
                yield
            for k, te in enumerate(tes):
                ts[k] = ts[k] - _dot(te.astype(BF16), bd(ts[k]))
                yield
            s *= 2
        uws = []
        for r, t in zip(rows, ts):
            beta, gc = beta_s[r, :], gc_s[r, :]
            rhs = jnp.concatenate([bd(v_s[r, :] * beta), bd(k_s[r, :] * beta * jnp.exp(gc))], axis=1)
            uws.append(_dot(t.astype(BF16), rhs))
            yield
        for c, r, uw in zip(chunks, rows, uws):
            u, w = uw[:, :WG], uw[:, WG:]
            tr = _dot(jnp.concatenate([kd_s[r, :], qk_s[r, :]], axis=0), jnp.concatenate([bd(w), bd(u)], axis=1))
            yield
            q_eff = q_s[r, :] * jnp.exp(gc_s[r, :]) - tr[CHUNK:, :WG]
            lhs_s[pl.ds(pl.multiple_of(c * 2 * CHUNK, 2 * CHUNK), CHUNK), :] = q_eff.astype(BF16)
            lhs_s[pl.ds(pl.multiple_of(c * 2 * CHUNK + CHUNK, CHUNK), CHUNK), :] = tr[:CHUNK, :WG].astype(BF16)
            kdu_s[r, :] = tr[:CHUNK, WG:]
            o_s[r, :] = tr[CHUNK:, WG:]

    def scan_stages(first, count):
        state = state_s[...]
        for k in range(count):
            c = first + k
            rows = pl.ds(pl.multiple_of(c * CHUNK, CHUNK), CHUNK)
            ws = _dot(lhs_s[pl.ds(pl.multiple_of(c * 2 * CHUNK, 2 * CHUNK), 2 * CHUNK), :], bd(state))
            yield
            gl = gc_s[pl.ds(pl.multiple_of(c * CHUNK + CHUNK - 8, 8), 8), :][7:8, :]
            state = state * jnp.exp(gl) - ws[CHUNK:] + kdu_s[rows, :]
            o_s[rows, :] = o_s[rows, :] + ws[:CHUNK]
        state_s[...] = state

    def drain(stages):
        for _ in stages:
            pass

    def interleave(main, side, main_per_side):
        main_live = side_live = True
        while main_live or side_live:
            if side_live:
                side_live = next(side, 0) is None
            for _ in range(main_per_side):
                if main_live:
                    main_live = next(main, 0) is None

    n_groups = seq // (CHUNK * GDN_INTERLEAVE)
    intra_matmuls = GDN_INTERLEAVE * (3 + 2 * (CHUNK.bit_length() - 2))
    scan_matmuls = GDN_INTERLEAVE

    pairs_per_group = GDN_INTERLEAVE * CHUNK // (2 * PREP_ROWS)
    for i in range(pairs_per_group):
        drain(prep_stages(i))
    later = [prep_stages(i) for i in range(pairs_per_group, seq // (2 * PREP_ROWS))]
    for k, _ in enumerate(intra_stages(0)):
        if later and k >= GDN_INTERLEAVE and k % 3 == 0 and next(later[0], 0) is not None:
            later.pop(0)
    for stages in later:
        drain(stages)

    def merged(g, _):
        interleave(intra_stages(g), scan_stages((g - 1) * GDN_INTERLEAVE, GDN_INTERLEAVE),
                   intra_matmuls // scan_matmuls)
        return 0

    lax.fori_loop(1, n_groups, merged, 0)

    def post(r):
        rows = pl.ds(pl.multiple_of(r * PREP_ROWS, PREP_ROWS), PREP_ROWS)
        o = o_s[rows, :]
        ms = _dot((o * o).astype(BF16), gsum) * (1.0 / HD)
        gate = gate_ref[rows, :]
        y = o * lax.rsqrt(ms + EPS) * gout_ref[...] * (gate * _sigmoid(gate))
        o_ref[rows, :] = y.astype(o_ref.dtype)

    n_tiles = seq // PREP_ROWS
    done_tiles = (n_groups - 1) * GDN_INTERLEAVE * CHUNK // PREP_ROWS
    overlapped = min(done_tiles, GDN_INTERLEAVE)
    last_first = (n_groups - 1) * GDN_INTERLEAVE

    def last_scan_with_post(i, _):
        post(i)
        drain(scan_stages(last_first + i, 1))
        return 0

    def last_scan(c, _):
        drain(scan_stages(c, 1))
        return 0

    def post_only(r, _):
        post(r)
        return 0

    lax.fori_loop(0, overlapped, last_scan_with_post, 0)
    lax.fori_loop(last_first + overlapped, last_first + GDN_INTERLEAVE, last_scan, 0)
    lax.fori_loop(overlapped, n_tiles, post_only, 0)


def _gdn(p, alog_b, dtb_b, gout, consts, batch, seq):
    n = batch * seq
    vec = lambda w: pl.BlockSpec((1, w), lambda b: (0, 0))
    full = lambda a: pl.BlockSpec(a.shape, lambda b: (0,) * a.ndim)
    return pl.pallas_call(
        _gdn_kernel,
        grid=(batch,),
        in_specs=[
            pl.BlockSpec((seq, 3 * WG), lambda b: (b, COL_QKV_A // (3 * WG))),
            pl.BlockSpec((seq, AB_PAD), lambda b: (b, COL_AB // AB_PAD)),
            pl.BlockSpec((seq, WG), lambda b: (b, COL_GATE_A // WG)),
            vec(WG), vec(WG), vec(WG),
            full(consts["gsum"]), full(consts["expand"]), full(consts["ltri"]),
        ],
        out_specs=pl.BlockSpec((seq, WG), lambda b: (b, 0)),
        out_shape=jax.ShapeDtypeStruct((n, WG), MIX_OUT),
        scratch_shapes=[pltpu.VMEM((seq, WG), F32)] * 7 + [
            pltpu.VMEM((2 * seq, WG), BF16), pltpu.VMEM((seq, WG), BF16), pltpu.VMEM((seq, WG), BF16),
            pltpu.VMEM((CHUNK, WG), F32)],
        compiler_params=pltpu.CompilerParams(vmem_limit_bytes=VMEM_LIMIT),
        name="gated_deltanet",
    )(p, p, p, alog_b, dtb_b, gout, consts["gsum"], consts["expand"], consts["ltri"])


def _sb_kernel(qkv_ref, gout_ref, gsum_ref, sufsum_ref, o_ref, kbd_s, vbd_s, acc_s, carry_s):
    seq = qkv_ref.shape[0]
    gi = pl.program_id(1)
    blk = SB_QBLOCK

    @pl.when(gi == 0)
    def _():
        def build(j, _):
            rows = pl.ds(pl.multiple_of(j * blk, blk), blk)
            kt = qkv_ref[rows, WG:2 * WG].astype(F32).T
            row_head = lax.broadcasted_iota(jnp.int32, kt.shape, 0) // HD
            kbd_s[j] = jnp.concatenate([jnp.where(row_head == h, kt, 0.0) for h in range(HEADS)],
                                       axis=1).astype(BF16)
            vbd_s[j] = _blockdiag(qkv_ref[rows, 2 * WG:].astype(F32), HD).astype(BF16)
            return 0
        lax.fori_loop(0, seq // blk, build, 0)

    sufsum = sufsum_ref[...]
    past = lax.broadcasted_iota(jnp.int32, (blk, blk), 1) < lax.broadcasted_iota(jnp.int32, (blk, blk), 0)
    qblocks = [gi * SB_GROUP + k for k in range(SB_GROUP)]
    qs = [(qkv_ref[pl.ds(pl.multiple_of(qb * blk, blk), blk), :WG].astype(F32) * (LOG2_E * HD ** -0.5)).astype(BF16)
          for qb in qblocks]

    def step(offset, diagonal):
        js = [jnp.maximum(qb - offset, 0) for qb in qblocks]
        zs = [_dot(q, kbd_s[j]) for q, j in zip(qs, js)]
        probs = []
        for k, z in enumerate(zs):
            rows = slice(k * blk, (k + 1) * blk)
            lss, nlfs = [], []
            for h in range(HEADS):
                zh = z[:, h * blk:(h + 1) * blk]
                nlf = jnp.maximum(zh, 0.0) + jnp.log2(1.0 + jnp.exp2(-jnp.abs(zh)))
                lss.append(zh - nlf)
                nlfs.append((jnp.where(past, nlf, 0.0) if diagonal else nlf).astype(BF16))
            cs = _dot(jnp.concatenate(nlfs, axis=0), sufsum)
            per_head = []
            for h in range(HEADS):
                lanes = slice(h * blk, (h + 1) * blk)
                csh = cs[h * blk:(h + 1) * blk, :]
                carry = carry_s[rows, lanes]
                a = jnp.exp2(lss[h] + csh[:, :blk] + carry)
                if diagonal:
                    a = jnp.where(past, a, 0.0)
                carry_s[rows, lanes] = carry + csh[:, blk:]
                per_head.append(a.astype(BF16))
            probs.append(jnp.concatenate(per_head, axis=1))
        for k, (p, j) in enumerate(zip(probs, js)):
            rows = slice(k * blk, (k + 1) * blk)
            av = _dot(p, vbd_s[j])
            if not diagonal:
                av = jnp.where(qblocks[k] >= offset, av, 0.0)
            acc_s[rows, :] += av

    def largest_live_carry(offset):
        live = jnp.float32(-jnp.inf)
        for k, qb in enumerate(qblocks):
            live = jnp.maximum(live, jnp.where(qb >= offset, jnp.max(carry_s[k * blk:(k + 1) * blk, :]), -jnp.inf))
        return live

    acc_s[...] = jnp.zeros_like(acc_s)
    carry_s[...] = jnp.zeros_like(carry_s)
    step(0, True)

    def cond(state):
        offset, largest = state
        return jnp.logical_and(offset <= qblocks[-1], largest > SB_ZERO_LOG2)

    def body(state):
        offset, _ = state
        step(offset, False)
        return offset + 1, largest_live_carry(offset + 1)

    lax.while_loop(cond, body, (jnp.int32(1), largest_live_carry(1)))

    o = acc_s[...]
    ms = _dot((o * o).astype(BF16), gsum_ref[...]) * (1.0 / HD)
    o_ref[...] = (o * lax.rsqrt(ms + EPS) * gout_ref[...]).astype(o_ref.dtype)


def _stick_breaking(p, gout, consts, batch, seq):
    n = batch * seq
    nq = seq // SB_QBLOCK
    ng = nq // SB_GROUP
    rows = SB_GROUP * SB_QBLOCK
    full = lambda a: pl.BlockSpec(a.shape, lambda b, i: (0,) * a.ndim)
    return pl.pallas_call(
        _sb_kernel,
        grid=(batch, ng),
        in_specs=[
            pl.BlockSpec((seq, 3 * WG), lambda b, i: (b, 0)),
            pl.BlockSpec((1, WG), lambda b, i: (0, 0)),
            full(consts["gsum"]), full(consts["sufsum"]),
        ],
        out_specs=pl.BlockSpec((rows, WG), lambda b, i: (b * ng + i, 0)),
        out_shape=jax.ShapeDtypeStruct((n, WG), MIX_OUT),
        scratch_shapes=[
            pltpu.VMEM((nq, WG, HEADS * SB_QBLOCK), BF16),
            pltpu.VMEM((nq, HEADS * SB_QBLOCK, WG), BF16),
            pltpu.VMEM((rows, WG), F32),
            pltpu.VMEM((rows, HEADS * SB_QBLOCK), F32),
        ],
        compiler_params=pltpu.CompilerParams(
            dimension_semantics=("arbitrary", "arbitrary"), vmem_limit_bytes=VMEM_LIMIT),
        name="stick_breaking",
    )(p, gout, consts["gsum"], consts["sufsum"])


def _outmlp_kernel(x_ref, ya_ref, yb_ref, yc_ref, yd_ref, mod_ref, g2_ref, gfin_ref,
                   wout_ref, w1_ref, w2_ref, o_ref, *, ff_chunk, final):
    m = mod_ref[0]
    sub = x_ref.shape[0] // TOKEN_SUBTILES
    parts = [slice(k * sub, (k + 1) * sub) for k in range(TOKEN_SUBTILES)]
    ys = [jnp.concatenate([ref[r, :].astype(BF16) for ref in (ya_ref, yb_ref, yc_ref, yd_ref)], axis=1)
          for r in parts]
    xs = [x_ref[r, :] + m[2:3] * _dot(y, wout_ref[...]) for r, y in zip(parts, ys)]
    hs = [(_rms(x, g2_ref[...]) * (1.0 + m[4:5]) + m[3:4]).astype(BF16) for x in xs]
    d_ff = w1_ref.shape[1]
    ffs = [jnp.zeros(x.shape, F32) for x in xs]
    for k in range(0, d_ff, ff_chunk):
        acts = [jnp.maximum(_dot(h, w1_ref[:, k:k + ff_chunk]), 0.0) for h in hs]
        ffs = [ff + _dot((a * a).astype(BF16), w2_ref[k:k + ff_chunk, :]) for ff, a in zip(ffs, acts)]
    for r, x, ff in zip(parts, xs, ffs):
        x = x + m[5:6] * ff
        if final:
            x = _rms(x, gfin_ref[...])
        o_ref[r, :] = x


def _outmlp(x2, ys, mod_l, g2, gfin, w_out, w_ff1, w_ff2, layer, seq, tm, final):
    n, d = x2.shape
    d_ff = w_ff1.shape[-1]
    tiles_per_seq = seq // tm
    tile = lambda w: pl.BlockSpec((tm, w), lambda i: (i, 0))
    const = lambda a: pl.BlockSpec((None,) + a.shape[1:], lambda i: (layer, 0, 0), pipeline_mode=pl.Buffered(1))
    return pl.pallas_call(
        functools.partial(_outmlp_kernel, ff_chunk=min(d_ff, 1024), final=final),
        grid=(n // tm,),
        in_specs=[
            tile(d), tile(WG), tile(WG), tile(WG), tile(WG),
            pl.BlockSpec((1, N_MOD, d), lambda i: (i // tiles_per_seq, 0, 0)),
            pl.BlockSpec((1, d), lambda i: (0, 0)),
            pl.BlockSpec((1, d), lambda i: (0, 0)),
            const(w_out), const(w_ff1), const(w_ff2),
        ],
        out_specs=tile(d),
        out_shape=jax.ShapeDtypeStruct((n, d), F32),
        compiler_params=pltpu.CompilerParams(vmem_limit_bytes=VMEM_LIMIT),
        name="outproj_mlp",
    )(x2, *ys, mod_l, g2, gfin, w_out, w_ff1, w_ff2)


def _constants():
    idx = jnp.arange(WG)
    gsum = (idx[:, None] // HD == idx[None, :] // HD).astype(BF16)
    ab = jnp.arange(AB_PAD)[:, None]
    lane = jnp.arange(2 * WG)[None, :]
    expand = (ab == (lane // WG) * HEADS + (lane % WG) // HD).astype(BF16)
    r = jnp.arange(PREP_ROWS)
    ltri = ((r[:, None] // CHUNK == r[None, :] // CHUNK) & (r[:, None] >= r[None, :])).astype(BF16)
    k = jnp.arange(SB_QBLOCK)
    suf = (k[:, None] > k[None, :]).astype(BF16)
    sufsum = -jnp.concatenate([suf, jnp.ones((SB_QBLOCK, SB_QBLOCK), BF16)], axis=1)
    return {"gsum": gsum, "expand": expand, "ltri": ltri, "sufsum": sufsum}


def _pad_w_in(w_in):
    o = 0
    parts = {}
    for name, size in (("qkv_a", 3 * WG), ("ab", 2 * HEADS), ("gate_a", WG), ("u_b", WG), ("v_b", WG),
                       ("x_c", WG), ("qkv_d", 3 * WG)):
        parts[name] = w_in[..., o:o + size]
        o += size
    ab = jnp.pad(parts["ab"], ((0, 0), (0, 0), (0, AB_PAD - 2 * HEADS)))
    return jnp.concatenate([parts["qkv_a"], parts["gate_a"], ab, parts["qkv_d"], parts["u_b"], parts["v_b"],
                            parts["x_c"]], axis=-1).astype(BF16)


def _per_head_lanes(v):
    return jnp.repeat(v, HD, axis=-1)[..., None, :]


def kernel(x, c, w_mod, b_mod, g_norm1, g_norm2, w_in, conv_a, a_log, dt_bias, g_out_a, g_ln_b, w_sg, b_sg,
           w_pool, s_pool, g_out_d, w_out, w_ff1, w_ff2, g_final):
    batch, seq, d = x.shape
    depth = w_mod.shape[0]
    assert seq % (SB_GROUP * SB_QBLOCK) == 0 and seq % (GDN_INTERLEAVE * CHUNK) == 0 and d == HEADS * WG
    tm = min(512, seq)
    consts = _constants()

    mod = _modulation(c, w_mod, b_mod).reshape(depth, batch, N_MOD, d)
    w_in_p = _pad_w_in(w_in)
    w_out_b, w_ff1_b, w_ff2_b = w_out.astype(BF16), w_ff1.astype(BF16), w_ff2.astype(BF16)
    alog_b, dtb_b = _per_head_lanes(a_log), _per_head_lanes(dt_bias)
    wsg_cat = jnp.transpose(w_sg, (0, 2, 1, 3)).reshape(depth, SG_BLOCK, HEADS * SG_BLOCK)
    bsg_packed = jnp.repeat(jnp.transpose(b_sg, (0, 2, 1)), HD, axis=-1)
    wpool_bd = (jnp.eye(HEADS, dtype=F32)[None, :, None, :, None] * w_pool[:, :, :, None, :]).reshape(
        depth, WG, WG).astype(BF16)
    row = lambda v: v[:, None, :]

    x2 = x.reshape(batch * seq, d)
    for l in range(depth):
        sg = (row(g_ln_b)[l], wsg_cat[l], bsg_packed[l], wpool_bd[l], row(s_pool)[l])
        p, qkv_d, y_b, y_c = _inproj(x2, mod[l], row(g_norm1)[l], w_in_p, l, conv_a[l], sg, consts, seq, tm)
        y_a = _gdn(p, alog_b[l], dtb_b[l], row(g_out_a)[l], consts, batch, seq)
        y_d = _stick_breaking(qkv_d, row(g_out_d)[l], consts, batch, seq)
        x2 = _outmlp(x2, (y_a, y_b, y_c, y_d), mod[l], row(g_norm2)[l], g_final[None, :], w_out_b,
                     w_ff1_b, w_ff2_b, l, seq, tm, final=(l == depth - 1))
    return x2.reshape(batch, seq, d)
```

```python
import functools

import jax
import jax.numpy as jnp
from jax import lax
from jax.experimental import pallas as pl
from jax.experimental.pallas import tpu as pltpu

F32 = jnp.float32
BF16 = jnp.bfloat16
MIX_OUT = BF16

EPS = 1e-6
N_MOD = 6
HEADS = 4
HD = 64
WG = HEADS * HD
CHUNK = 64
CONV_W = 4
CONV_HALO = 8
SG_BLOCK = 128
SB_QBLOCK = 128
POOL_WINDOWS = (2, 4, 8, 16)
POOL_HALO = 16
TOKEN_SUBTILES = 2
PREP_ROWS = 256
GDN_INTERLEAVE = 8
SB_GROUP = 16
LOG2_E = 1.4426950408889634
SB_ZERO_LOG2 = -160.0

COL_QKV_A, COL_GATE_A, COL_AB, COL_QKV_D, COL_U_B, COL_V_B, COL_X_C = 0, 768, 1024, 1152, 1920, 2176, 2432
AB_PAD = 128
P_GDN = COL_QKV_D
P_SB = COL_U_B
P_PAD = COL_X_C + 256

VMEM_LIMIT = 56 * 1024 * 1024


def _dot(a, b):
    return jnp.dot(a, b, preferred_element_type=F32)


def _dot_nt(a, b):
    return lax.dot_general(a, b, (((1,), (1,)), ((), ())), preferred_element_type=F32)


def _split(x):
    hi = x.astype(BF16)
    lo = (x - hi.astype(F32)).astype(BF16)
    return hi, lo


def _dot_exact_rhs(x, m):
    hi, lo = _split(x)
    return _dot(hi, m) + _dot(lo, m)


def _dot_exact_lhs(m, x):
    hi, lo = _split(x)
    return _dot(m, hi) + _dot(m, lo)


def _sigmoid(x):
    return 1.0 / (1.0 + jnp.exp(-x))


def _softplus(x):
    return jnp.maximum(x, 0.0) + jnp.log(1.0 + jnp.exp(-jnp.abs(x)))


def _lane_head(shape):
    return lax.broadcasted_iota(jnp.int32, shape, 1) // HD


def _blockdiag(y, width):
    lane_blk = lax.broadcasted_iota(jnp.int32, y.shape, 1) // width
    return jnp.concatenate([jnp.where(lane_blk == h, y, 0.0) for h in range(HEADS)], axis=0)


def _shift_rows(x, k):
    return pltpu.roll(x, k, axis=0)


def _mod_kernel(c_ref, w_ref, b_ref, o_ref):
    c = c_ref[...]
    cond = c * _sigmoid(c)
    o_ref[0] = _dot(cond.astype(BF16), w_ref[0].astype(BF16)) + b_ref[0]


def _modulation(c, w_mod, b_mod):
    depth, d, n6 = w_mod.shape
    b = c.shape[0]
    nblk = n6 // d
    return pl.pallas_call(
        _mod_kernel,
        grid=(depth, nblk),
        in_specs=[
            pl.BlockSpec((b, d), lambda l, j: (0, 0)),
            pl.BlockSpec((1, d, d), lambda l, j: (l, 0, j)),
            pl.BlockSpec((1, 1, d), lambda l, j: (l, 0, j)),
        ],
        out_specs=pl.BlockSpec((1, b, d), lambda l, j: (l, 0, j)),
        out_shape=jax.ShapeDtypeStruct((depth, b, n6), F32),
        name="modulation",
    )(c, w_mod, b_mod.reshape(depth, 1, n6))


def _rms(x, g):
    return x * lax.rsqrt(jnp.mean(x * x, axis=-1, keepdims=True) + EPS) * g


def _sg_pool_stages(rows, u, v, xc, halo, t0, gln_ref, wsg_ref, bsg_ref, wpool_ref, spool_ref, gsum, yb_ref,
                    yc_ref):
    blocks = [slice(k * SG_BLOCK, (k + 1) * SG_BLOCK) for k in range(rows // SG_BLOCK)]
    t_chunk = lax.broadcasted_iota(jnp.int32, wsg_ref.shape, 0) // CHUNK
    s_chunk = (lax.broadcasted_iota(jnp.int32, wsg_ref.shape, 1) % SG_BLOCK) // CHUNK
    wm = jnp.where(t_chunk >= s_chunk, wsg_ref[...], 0.0).astype(BF16)
    whole = slice(0, rows)
    mean = _dot(v(whole).astype(BF16), gsum) * (1.0 / HD)
    yield
    vc = v(whole) - mean
    var = _dot((vc * vc).astype(BF16), gsum) * (1.0 / HD)
    lane_grp = _lane_head((PREP_ROWS, WG))
    win = jnp.zeros((PREP_ROWS, WG), jnp.int32)
    for g, w in enumerate(POOL_WINDOWS):
        win = jnp.where(lane_grp == g, w, win)
    pooled = []
    for piece in range(rows // PREP_ROWS):
        cur = xc(slice(piece * PREP_ROWS, (piece + 1) * PREP_ROWS))
        prev = halo if piece == 0 else xc(slice(piece * PREP_ROWS - POOL_HALO, piece * PREP_ROWS))
        s = jnp.concatenate([prev, cur], axis=0)
        total = jnp.zeros((PREP_ROWS, WG), F32)
        k = 1
        for g, w in enumerate(POOL_WINDOWS):
            while k < w:
                s = s + _shift_rows(s, k)
                k *= 2
            total = jnp.where(lane_grp == g, s[POOL_HALO:, :], total)
        t = t0 + piece * PREP_ROWS + lax.broadcasted_iota(jnp.int32, (PREP_ROWS, WG), 0)
        count = jnp.minimum(t + 1, win).astype(F32)
        pooled.append((total / count - cur).astype(BF16))
    yield
    vn = vc * lax.rsqrt(var + EPS) * gln_ref[...]
    vns = [_blockdiag(vn[r, :], HD).astype(BF16) for r in blocks]
    for r, vn in zip(blocks, vns):
        yb_ref[r, :] = (u(r) * (_dot(wm, vn) + bsg_ref[...])).astype(yb_ref.dtype)
    for piece, pl_ in enumerate(pooled):
        yc_ref[piece * PREP_ROWS:(piece + 1) * PREP_ROWS, :] = (
            _dot(pl_, wpool_ref[...]) * spool_ref[...]).astype(yc_ref.dtype)


def _inproj_kernel(x_ref, mod_ref, g_ref, w_ref, convw_ref, gln_ref, wsg_ref, bsg_ref, wpool_ref, spool_ref,
                   gsum_ref, o_ref, osb_ref, yb_ref, yc_ref, uvx_s, halo_s, chalo_s, *, tiles_per_seq, n_tiles):
    i = pl.program_id(0)
    tm = x_ref.shape[0]

    @pl.when(i == 0)
    def _():
        uvx_s[...] = jnp.zeros_like(uvx_s)
        halo_s[...] = jnp.zeros_like(halo_s)
        chalo_s[...] = jnp.zeros_like(chalo_s)

    prev_tile = jnp.maximum(i - 1, 0)
    prev_buf = (i + 1) % 2
    u, v, xc = (functools.partial(lambda k, r: uvx_s[prev_buf, r, k * WG:(k + 1) * WG], k) for k in range(3))
    halo = jnp.where(prev_tile % tiles_per_seq == 0, 0.0, halo_s[...])
    t0 = (prev_tile % tiles_per_seq) * tm
    side = _sg_pool_stages(tm, u, v, xc, halo, t0, gln_ref, wsg_ref, bsg_ref, wpool_ref, spool_ref, gsum_ref[...],
                           yb_ref, yc_ref)

    m = mod_ref[0]
    sub = tm // TOKEN_SUBTILES
    parts = [slice(k * sub, (k + 1) * sub) for k in range(TOKEN_SUBTILES)]
    hs = [(_rms(x_ref[r, :], g_ref[...]) * (1.0 + m[1:2]) + m[0:1]).astype(BF16) for r in parts]
    next(side)
    new_halo = xc(slice(tm - POOL_HALO, tm))
    tile = jnp.minimum(i, n_tiles - 1)
    conv_prev = jnp.where(tile % tiles_per_seq == 0, 0.0, chalo_s[(tile + 1) % 2])
    for r, h in zip(parts, hs):
        p = _dot(h, w_ref[...])
        qa = p[:, :3 * WG]
        ext = jnp.concatenate([conv_prev, qa], axis=0)
        acc = qa * convw_ref[CONV_W - 1:CONV_W, :]
        for k in range(1, CONV_W):
            acc = acc + _shift_rows(ext, k)[CONV_HALO:, :] * convw_ref[CONV_W - 1 - k:CONV_W - k, :]
        o_ref[r, :3 * WG] = acc * _sigmoid(acc)
        o_ref[r, 3 * WG:] = p[:, 3 * WG:P_GDN]
        osb_ref[r, :] = p[:, P_GDN:P_SB].astype(osb_ref.dtype)
        uvx_s[i % 2, r, :] = p[:, P_SB:]
        conv_prev = qa[sub - CONV_HALO:, :]
        next(side, None)
    for _ in side:
        pass
    halo_s[...] = new_halo
    chalo_s[tile % 2] = conv_prev


def _inproj(x2, mod_l, g1, w_in_p, layer, conv_w, sg, consts, seq, tm):
    n, d = x2.shape
    tiles_per_seq = seq // tm
    n_tiles = n // tm
    cur = lambda i: jnp.minimum(i, n_tiles - 1)
    full = lambda a: pl.BlockSpec(a.shape, lambda i: (0,) * a.ndim)
    mix_out = pl.BlockSpec((tm, WG), lambda i: (jnp.maximum(i - 1, 0), 0))
    return pl.pallas_call(
        functools.partial(_inproj_kernel, tiles_per_seq=tiles_per_seq, n_tiles=n_tiles),
        grid=(n_tiles + 1,),
        in_specs=[
            pl.BlockSpec((tm, d), lambda i: (cur(i), 0)),
            pl.BlockSpec((1, N_MOD, d), lambda i: (cur(i) // tiles_per_seq, 0, 0)),
            pl.BlockSpec((1, d), lambda i: (0, 0)),
            pl.BlockSpec((None, d, P_PAD), lambda i: (layer, 0, 0)),
        ] + [full(conv_w)] + [full(a) for a in sg] + [full(consts["gsum"])],
        out_specs=[pl.BlockSpec((tm, P_GDN), lambda i: (cur(i), 0)),
                   pl.BlockSpec((tm, P_SB - P_GDN), lambda i: (cur(i), 0)), mix_out, mix_out],
        out_shape=[jax.ShapeDtypeStruct((n, P_GDN), F32), jax.ShapeDtypeStruct((n, P_SB - P_GDN), BF16)]
        + [jax.ShapeDtypeStruct((n, WG), MIX_OUT)] * 2,
        scratch_shapes=[pltpu.VMEM((2, tm, 3 * WG), F32), pltpu.VMEM((POOL_HALO, WG), F32),
                        pltpu.VMEM((2, CONV_HALO, 3 * WG), F32)],
        compiler_params=pltpu.CompilerParams(dimension_semantics=("arbitrary",), vmem_limit_bytes=VMEM_LIMIT),
        name="inproj_gating_pool",
    )(x2, mod_l, g1, w_in_p, conv_w, *sg, consts["gsum"])


def _gdn_kernel(qkv_ref, ab_ref, gate_ref, alog_ref, dtb_ref, gout_ref,
                gsum_ref, expand_ref, ltri_ref, o_ref,
                q_s, k_s, v_s, beta_s, gc_s, o_s, kdu_s, lhs_s, qk_s, kd_s, state_s):
    seq = qkv_ref.shape[0]
    gsum = gsum_ref[...]

    def prep_stages(i):
        tiles = [slice((2 * i + t) * PREP_ROWS, (2 * i + t + 1) * PREP_ROWS) for t in range(2)]
        ys = [qkv_ref[rows, :] for rows in tiles]
        qs = [y[:, :WG] for y in ys]
        ks = [y[:, WG:2 * WG] for y in ys]
        qss = [_dot((q * q).astype(BF16), gsum) for q in qs]
        kss = [_dot((kk * kk).astype(BF16), gsum) for kk in ks]
        abxs = [_dot_exact_rhs(ab_ref[rows, :], expand_ref[...]) for rows in tiles]
        yield
        gs = [-jnp.exp(alog_ref[...]) * _softplus(abx[:, :WG] + dtb_ref[...]) for abx in abxs]
        gcs = [_dot_exact_lhs(ltri_ref[...], g) for g in gs]
        yield
        for rows, y, q, kk, q2, k2, abx, gc in zip(tiles, ys, qs, ks, qss, kss, abxs, gcs):
            q_s[rows, :] = q * lax.rsqrt(q2 + EPS) * (HD ** -0.5)
            k_s[rows, :] = kk * lax.rsqrt(k2 + EPS)
            v_s[rows, :] = y[:, 2 * WG:]
            beta_s[rows, :] = _sigmoid(abx[:, WG:])
            gc_s[rows, :] = gc

    state_s[...] = jnp.zeros_like(state_s)
    shape = (CHUNK, WG)
    row = lax.broadcasted_iota(jnp.int32, shape, 0)
    col = lax.broadcasted_iota(jnp.int32, shape, 1) % CHUNK
    eye = row == col
    incl = row >= col
    strict = row > col

    def bd(y):
        return _blockdiag(y, HD).astype(BF16)

    def intra_stages(i):
        chunks = [i * GDN_INTERLEAVE + k for k in range(GDN_INTERLEAVE)]
        rows = [pl.ds(pl.multiple_of(c * CHUNK, CHUNK), CHUNK) for c in chunks]
        lows, ts = [], []
        for c, r in zip(chunks, rows):
            qn, kn, gc = q_s[r, :], k_s[r, :], gc_s[r, :]
            kb = kn * beta_s[r, :]
            a = _dot_nt(jnp.concatenate([kb, qn], axis=0).astype(BF16), bd(kn))
            yield
            gc_row = jnp.sum(jnp.where(eye, gc, 0.0), axis=0, keepdims=True)
            decay = jnp.where(incl, jnp.exp(gc - gc_row), 0.0)
            low = jnp.where(strict, a[:CHUNK] * decay, 0.0)
            qk_s[r, :] = (a[CHUNK:] * decay).astype(BF16)
            kd_t = (kn * jnp.exp(gc[CHUNK - 1:CHUNK, :] - gc)).T
            kd_s[r, :] = jnp.concatenate([kd_t[h * HD:(h + 1) * HD, :] for h in range(HEADS)],
                                         axis=1).astype(BF16)
            lows.append(low)
            ts.append(jnp.where(eye, 1.0, 0.0) - jnp.where(row // 2 == col // 2, low, 0.0))
        s = 2
        while s < CHUNK:
            join = jnp.logical_and(row // (2 * s) == col // (2 * s), row // s != col // s)
            tes = []
            for t, low in zip(ts, lows):
                tes.append(_dot(t.astype(BF16), bd(jnp.where(join, low, 0.0))))
                yield
            for k, te in enumerate(tes):
                ts[k] = ts[k] - _dot(te.astype(BF16), bd(ts[k]))
                yield
            s *= 2
        uws = []
        for r, t in zip(rows, ts):
            beta, gc = beta_s[r, :], gc_s[r, :]
            rhs = jnp.concatenate([bd(v_s[r, :] * beta), bd(k_s[r, :] * beta * jnp.exp(gc))], axis=1)
            uws.append(_dot(t.astype(BF16), rhs))
            yield
        for c, r, uw in zip(chunks, rows, uws):
            u, w = uw[:, :WG], uw[:, WG:]
            tr = _dot(jnp.concatenate([kd_s[r, :], qk_s[r, :]], axis=0), jnp.concatenate([bd(w), bd(u)], axis=1))
            yield
            q_eff = q_s[r, :] * jnp.exp(gc_s[r, :]) - tr[CHUNK:, :WG]
            lhs_s[pl.ds(pl.multiple_of(c * 2 * CHUNK, 2 * CHUNK), CHUNK), :] = q_eff.astype(BF16)
            lhs_s[pl.ds(pl.multiple_of(c * 2 * CHUNK + CHUNK, CHUNK), CHUNK), :] = tr[:CHUNK, :WG].astype(BF16)
            kdu_s[r, :] = tr[:CHUNK, WG:]
            o_s[r, :] = tr[CHUNK:, WG:]

    def scan_stages(first, count):
        state = state_s[...]
        for k in range(count):
            c = first + k
            rows = pl.ds(pl.multiple_of(c * CHUNK, CHUNK), CHUNK)
            ws = _dot(lhs_s[pl.ds(pl.multiple_of(c * 2 * CHUNK, 2 * CHUNK), 2 * CHUNK), :], bd(state))
            yield
            gl = gc_s[pl.ds(pl.multiple_of(c * CHUNK + CHUNK - 8, 8), 8), :][7:8, :]
            state = state * jnp.exp(gl) - ws[CHUNK:] + kdu_s[rows, :]
            o_s[rows, :] = o_s[rows, :] + ws[:CHUNK]
        state_s[...] = state

    def drain(stages):
        for _ in stages:
            pass

    def interleave(main, side, main_per_side):
        main_live = side_live = True
        while main_live or side_live:
            if side_live:
                side_live = next(side, 0) is None
            for _ in range(main_per_side):
                if main_live:
                    main_live = next(main, 0) is None

    n_groups = seq // (CHUNK * GDN_INTERLEAVE)
    intra_matmuls = GDN_INTERLEAVE * (3 + 2 * (CHUNK.bit_length() - 2))
    scan_matmuls = GDN_INTERLEAVE

    pairs_per_group = GDN_INTERLEAVE * CHUNK // (2 * PREP_ROWS)
    for i in range(pairs_per_group):
        drain(prep_stages(i))
    later = [prep_stages(i) for i in range(pairs_per_group, seq // (2 * PREP_ROWS))]
    for k, _ in enumerate(intra_stages(0)):
        if later and k >= GDN_INTERLEAVE and k % 3 == 0 and next(later[0], 0) is not None:
            later.pop(0)
    for stages in later:
        drain(stages)

    def merged(g, _):
        interleave(intra_stages(g), scan_stages((g - 1) * GDN_INTERLEAVE, GDN_INTERLEAVE),
                   intra_matmuls // scan_matmuls)
        return 0

    lax.fori_loop(1, n_groups, merged, 0)

    def post(r):
        rows = pl.ds(pl.multiple_of(r * PREP_ROWS, PREP_ROWS), PREP_ROWS)
        o = o_s[rows, :]
        ms = _dot((o * o).astype(BF16), gsum) * (1.0 / HD)
        gate = gate_ref[rows, :]
        y = o * lax.rsqrt(ms + EPS) * gout_ref[...] * (gate * _sigmoid(gate))
        o_ref[rows, :] = y.astype(o_ref.dtype)

    n_tiles = seq // PREP_ROWS
    done_tiles = (n_groups - 1) * GDN_INTERLEAVE * CHUNK // PREP_ROWS
    overlapped = min(done_tiles, GDN_INTERLEAVE)
    last_first = (n_groups - 1) * GDN_INTERLEAVE

    def last_scan_with_post(i, _):
        post(i)
        drain(scan_stages(last_first + i, 1))
        return 0

    def last_scan(c, _):
        drain(scan_stages(c, 1))
        return 0

    def post_only(r, _):
        post(r)
        return 0

    lax.fori_loop(0, overlapped, last_scan_with_post, 0)
    lax.fori_loop(last_first + overlapped, last_first + GDN_INTERLEAVE, last_scan, 0)
    lax.fori_loop(overlapped, n_tiles, post_only, 0)


def _gdn(p, alog_b, dtb_b, gout, consts, batch, seq):
    n = batch * seq
    vec = lambda w: pl.BlockSpec((1, w), lambda b: (0, 0))
    full = lambda a: pl.BlockSpec(a.shape, lambda b: (0,) * a.ndim)
    return pl.pallas_call(
        _gdn_kernel,
        grid=(batch,),
        in_specs=[
            pl.BlockSpec((seq, 3 * WG), lambda b: (b, COL_QKV_A // (3 * WG))),
            pl.BlockSpec((seq, AB_PAD), lambda b: (b, COL_AB // AB_PAD)),
            pl.BlockSpec((seq, WG), lambda b: (b, COL_GATE_A // WG)),
            vec(WG), vec(WG), vec(WG),
            full(consts["gsum"]), full(consts["expand"]), full(consts["ltri"]),
        ],
        out_specs=pl.BlockSpec((seq, WG), lambda b: (b, 0)),
        out_shape=jax.ShapeDtypeStruct((n, WG), MIX_OUT),
        scratch_shapes=[pltpu.VMEM((seq, WG), F32)] * 7 + [
            pltpu.VMEM((2 * seq, WG), BF16), pltpu.VMEM((seq, WG), BF16), pltpu.VMEM((seq, WG), BF16),
            pltpu.VMEM((CHUNK, WG), F32)],
        compiler_params=pltpu.CompilerParams(vmem_limit_bytes=VMEM_LIMIT),
        name="gated_deltanet",
    )(p, p, p, alog_b, dtb_b, gout, consts["gsum"], consts["expand"], consts["ltri"])


def _sb_kernel(qkv_ref, gout_ref, gsum_ref, sufsum_ref, o_ref, kbd_s, vbd_s, acc_s, carry_s):
    seq = qkv_ref.shape[0]
    gi = pl.program_id(1)
    blk = SB_QBLOCK

    @pl.when(gi == 0)
    def _():
        def build(j, _):
            rows = pl.ds(pl.multiple_of(j * blk, blk), blk)
            kt = qkv_ref[rows, WG:2 * WG].astype(F32).T
            row_head = lax.broadcasted_iota(jnp.int32, kt.shape, 0) // HD
            kbd_s[j] = jnp.concatenate([jnp.where(row_head == h, kt, 0.0) for h in range(HEADS)],
                                       axis=1).astype(BF16)
            vbd_s[j] = _blockdiag(qkv_ref[rows, 2 * WG:].astype(F32), HD).astype(BF16)
            return 0
        lax.fori_loop(0, seq // blk, build, 0)

    sufsum = sufsum_ref[...]
    past = lax.broadcasted_iota(jnp.int32, (blk, blk), 1) < lax.broadcasted_iota(jnp.int32, (blk, blk), 0)
    qblocks = [gi * SB_GROUP + k for k in range(SB_GROUP)]
    qs = [(qkv_ref[pl.ds(pl.multiple_of(qb * blk, blk), blk), :WG].astype(F32) * (LOG2_E * HD ** -0.5)).astype(BF16)
          for qb in qblocks]

    def step(offset, diagonal):
        js = [jnp.maximum(qb - offset, 0) for qb in qblocks]
        zs = [_dot(q, kbd_s[j]) for q, j in zip(qs, js)]
        probs = []
        for k, z in enumerate(zs):
            rows = slice(k * blk, (k + 1) * blk)
            lss, nlfs = [], []
            for h in range(HEADS):
                zh = z[:, h * blk:(h + 1) * blk]
                nlf = jnp.maximum(zh, 0.0) + jnp.log2(1.0 + jnp.exp2(-jnp.abs(zh)))
                lss.append(zh - nlf)
                nlfs.append((jnp.where(past, nlf, 0.0) if diagonal else nlf).astype(BF16))
            cs = _dot(jnp.concatenate(nlfs, axis=0), sufsum)
            per_head = []
            for h in range(HEADS):
                lanes = slice(h * blk, (h + 1) * blk)
                csh = cs[h * blk:(h + 1) * blk, :]
                carry = carry_s[rows, lanes]
                a = jnp.exp2(lss[h] + csh[:, :blk] + carry)
                if diagonal:
                    a = jnp.where(past, a, 0.0)
                carry_s[rows, lanes] = carry + csh[:, blk:]
                per_head.append(a.astype(BF16))
            probs.append(jnp.concatenate(per_head, axis=1))
        for k, (p, j) in enumerate(zip(probs, js)):
            rows = slice(k * blk, (k + 1) * blk)
            av = _dot(p, vbd_s[j])
            if not diagonal:
                av = jnp.where(qblocks[k] >= offset, av, 0.0)
            acc_s[rows, :] += av

    def largest_live_carry(offset):
        live = jnp.float32(-jnp.inf)
        for k, qb in enumerate(qblocks):
            live = jnp.maximum(live, jnp.where(qb >= offset, jnp.max(carry_s[k * blk:(k + 1) * blk, :]), -jnp.inf))
        return live

    acc_s[...] = jnp.zeros_like(acc_s)
    carry_s[...] = jnp.zeros_like(carry_s)
    step(0, True)

    def cond(state):
        offset, largest = state
        return jnp.logical_and(offset <= qblocks[-1], largest > SB_ZERO_LOG2)

    def body(state):
        offset, _ = state
        step(offset, False)
        return offset + 1, largest_live_carry(offset + 1)

    lax.while_loop(cond, body, (jnp.int32(1), largest_live_carry(1)))

    o = acc_s[...]
    ms = _dot((o * o).astype(BF16), gsum_ref[...]) * (1.0 / HD)
    o_ref[...] = (o * lax.rsqrt(ms + EPS) * gout_ref[...]).astype(o_ref.dtype)


def _stick_breaking(p, gout, consts, batch, seq):
    n = batch * seq
    nq = seq // SB_QBLOCK
    ng = nq // SB_GROUP
    rows = SB_GROUP * SB_QBLOCK
    full = lambda a: pl.BlockSpec(a.shape, lambda b, i: (0,) * a.ndim)
    return pl.pallas_call(
        _sb_kernel,
        grid=(batch, ng),
        in_specs=[
            pl.BlockSpec((seq, 3 * WG), lambda b, i: (b, 0)),
            pl.BlockSpec((1, WG), lambda b, i: (0, 0)),
            full(consts["gsum"]), full(consts["sufsum"]),
        ],
        out_specs=pl.BlockSpec((rows, WG), lambda b, i: (b * ng + i, 0)),
        out_shape=jax.ShapeDtypeStruct((n, WG), MIX_OUT),
        scratch_shapes=[
            pltpu.VMEM((nq, WG, HEADS * SB_QBLOCK), BF16),
            pltpu.VMEM((nq, HEADS * SB_QBLOCK, WG), BF16),
            pltpu.VMEM((rows, WG), F32),
            pltpu.VMEM((rows, HEADS * SB_QBLOCK), F32),
        ],
        compiler_params=pltpu.CompilerParams(
            dimension_semantics=("arbitrary", "arbitrary"), vmem_limit_bytes=VMEM_LIMIT),
        name="stick_breaking",
    )(p, gout, consts["gsum"], consts["sufsum"])


def _outmlp_kernel(x_ref, ya_ref, yb_ref, yc_ref, yd_ref, mod_ref, g2_ref, gfin_ref,
                   wout_ref, w1_ref, w2_ref, o_ref, *, ff_chunk, final):
    m = mod_ref[0]
    sub = x_ref.shape[0] // TOKEN_SUBTILES
    parts = [slice(k * sub, (k + 1) * sub) for k in range(TOKEN_SUBTILES)]
    ys = [jnp.concatenate([ref[r, :].astype(BF16) for ref in (ya_ref, yb_ref, yc_ref, yd_ref)], axis=1)
          for r in parts]
    xs = [x_ref[r, :] + m[2:3] * _dot(y, wout_ref[...]) for r, y in zip(parts, ys)]
    hs = [(_rms(x, g2_ref[...]) * (1.0 + m[4:5]) + m[3:4]).astype(BF16) for x in xs]
    d_ff = w1_ref.shape[1]
    ffs = [jnp.zeros(x.shape, F32) for x in xs]
    for k in range(0, d_ff, ff_chunk):
        acts = [jnp.maximum(_dot(h, w1_ref[:, k:k + ff_chunk]), 0.0) for h in hs]
        ffs = [ff + _dot((a * a).astype(BF16), w2_ref[k:k + ff_chunk, :]) for ff, a in zip(ffs, acts)]
    for r, x, ff in zip(parts, xs, ffs):
        x = x + m[5:6] * ff
        if final:
            x = _rms(x, gfin_ref[...])
        o_ref[r, :] = x


def _outmlp(x2, ys, mod_l, g2, gfin, w_out, w_ff1, w_ff2, layer, seq, tm, final):
    n, d = x2.shape
    d_ff = w_ff1.shape[-1]
    tiles_per_seq = seq // tm
    tile = lambda w: pl.BlockSpec((tm, w), lambda i: (i, 0))
    const = lambda a: pl.BlockSpec((None,) + a.shape[1:], lambda i: (layer, 0, 0), pipeline_mode=pl.Buffered(1))
    return pl.pallas_call(
        functools.partial(_outmlp_kernel, ff_chunk=min(d_ff, 1024), final=final),
        grid=(n // tm,),
        in_specs=[
            tile(d), tile(WG), tile(WG), tile(WG), tile(WG),
            pl.BlockSpec((1, N_MOD, d), lambda i: (i // tiles_per_seq, 0, 0)),
            pl.BlockSpec((1, d), lambda i: (0, 0)),
            pl.BlockSpec((1, d), lambda i: (0, 0)),
            const(w_out), const(w_ff1), const(w_ff2),
        ],
        out_specs=tile(d),
        out_shape=jax.ShapeDtypeStruct((n, d), F32),
        compiler_params=pltpu.CompilerParams(vmem_limit_bytes=VMEM_LIMIT),
        name="outproj_mlp",
    )(x2, *ys, mod_l, g2, gfin, w_out, w_ff1, w_ff2)


def _constants():
    idx = jnp.arange(WG)
    gsum = (idx[:, None] // HD == idx[None, :] // HD).astype(BF16)
    ab = jnp.arange(AB_PAD)[:, None]
    lane = jnp.arange(2 * WG)[None, :]
    expand = (ab == (lane // WG) * HEADS + (lane % WG) // HD).astype(BF16)
    r = jnp.arange(PREP_ROWS)
    ltri = ((r[:, None] // CHUNK == r[None, :] // CHUNK) & (r[:, None] >= r[None, :])).astype(BF16)
    k = jnp.arange(SB_QBLOCK)
    suf = (k[:, None] > k[None, :]).astype(BF16)
    sufsum = -jnp.concatenate([suf, jnp.ones((SB_QBLOCK, SB_QBLOCK), BF16)], axis=1)
    return {"gsum": gsum, "expand": expand, "ltri": ltri, "sufsum": sufsum}


def _pad_w_in(w_in):
    o = 0
    parts = {}
    for name, size in (("qkv_a", 3 * WG), ("ab", 2 * HEADS), ("gate_a", WG), ("u_b", WG), ("v_b", WG),
                       ("x_c", WG), ("qkv_d", 3 * WG)):
        parts[name] = w_in[..., o:o + size]
        o += size
    ab = jnp.pad(parts["ab"], ((0, 0), (0, 0), (0, AB_PAD - 2 * HEADS)))
    return jnp.concatenate([parts["qkv_a"], parts["gate_a"], ab, parts["qkv_d"], parts["u_b"], parts["v_b"],
                            parts["x_c"]], axis=-1).astype(BF16)


def _per_head_lanes(v):
    return jnp.repeat(v, HD, axis=-1)[..., None, :]


def kernel(x, c, w_mod, b_mod, g_norm1, g_norm2, w_in, conv_a, a_log, dt_bias, g_out_a, g_ln_b, w_sg, b_sg,
           w_pool, s_pool, g_out_d, w_out, w_ff1, w_ff2, g_final):
    batch, seq, d = x.shape
    depth = w_mod.shape[0]
    assert seq % (SB_GROUP * SB_QBLOCK) == 0 and seq % (GDN_INTERLEAVE * CHUNK) == 0 and d == HEADS * WG
    tm = min(512, seq)
    consts = _constants()

    mod = _modulation(c, w_mod, b_mod).reshape(depth, batch, N_MOD, d)
    w_in_p = _pad_w_in(w_in)
    w_out_b, w_ff1_b, w_ff2_b = w_out.astype(BF16), w_ff1.astype(BF16), w_ff2.astype(BF16)
    alog_b, dtb_b = _per_head_lanes(a_log), _per_head_lanes(dt_bias)
    wsg_cat = jnp.transpose(w_sg, (0, 2, 1, 3)).reshape(depth, SG_BLOCK, HEADS * SG_BLOCK)
    bsg_packed = jnp.repeat(jnp.transpose(b_sg, (0, 2, 1)), HD, axis=-1)
    wpool_bd = (jnp.eye(HEADS, dtype=F32)[None, :, None, :, None] * w_pool[:, :, :, None, :]).reshape(
        depth, WG, WG).astype(BF16)
    row = lambda v: v[:, None, :]

    x2 = x.reshape(batch * seq, d)
    for l in range(depth):
        sg = (row(g_ln_b)[l], wsg_cat[l], bsg_packed[l], wpool_bd[l], row(s_pool)[l])
        p, qkv_d, y_b, y_c = _inproj(x2, mod[l], row(g_norm1)[l], w_in_p, l, conv_a[l], sg, consts, seq, tm)
        y_a = _gdn(p, alog_b[l], dtb_b[l], row(g_out_a)[l], consts, batch, seq)
        y_d = _stick_breaking(qkv_d, row(g_out_d)[l], consts, batch, seq)
        x2 = _outmlp(x2, (y_a, y_b, y_c, y_d), mod[l], row(g_norm2)[l], g_final[None, :], w_out_b,
                     w_ff1_b, w_ff2_b, l, seq, tm, final=(l == depth - 1))
    return x2.reshape(batch, seq, d)
```
